```python
import jax, jax.numpy as jnp
from jax import lax
import numpy as np

D_MODEL = 1024
BATCH = 8
SEQ = 8192
DEPTH = 1

HEAD_DIM = 64
MIX_WIDTH = D_MODEL
FOURIER_WIDTH = MIX_WIDTH // 2
ATTN_WIDTH = MIX_WIDTH - FOURIER_WIDTH
N_FOURIER_GROUPS = FOURIER_WIDTH // HEAD_DIM
N_SLOTS = ATTN_WIDTH // HEAD_DIM
DILATED_CONFIGS = ((128, 1), (512, 4), (2048, 16))
N_CFG = len(DILATED_CONFIGS)
N_ATTN_HEADS = N_SLOTS * N_CFG
QKV_WIDTH = N_ATTN_HEADS * HEAD_DIM
IN_WIDTH = 2 * FOURIER_WIDTH + 3 * QKV_WIDTH + ATTN_WIDTH
SPLIT_POINTS = tuple(int(p) for p in np.cumsum([FOURIER_WIDTH, FOURIER_WIDTH, QKV_WIDTH, QKV_WIDTH, QKV_WIDTH]))
NORM_EPS = 1e-6
MASK_VALUE = -1e30

kernel_name = "hymba_fnet_longnet_encoder_block"


def alibi_slopes(n):
    return 2.0 ** (-8.0 * jnp.arange(1, n + 1, dtype=jnp.float32) / n)


def rms_norm(t, w):
    t32 = t.astype(jnp.float32)
    return t32 * lax.rsqrt(jnp.mean(t32 * t32, axis=-1, keepdims=True) + NORM_EPS) * w.astype(jnp.float32)


def dilated_window_attention(q, k, v, window, dilation, slopes):
    B, S, H, DH = q.shape
    radius = window // (2 * dilation)
    blk = radius
    L = S // dilation
    N = B * dilation
    nb = -(-L // blk)
    Lp = nb * blk

    def to_sub(t):
        return t.reshape(B, L, dilation, H, DH).transpose(0, 2, 1, 3, 4).reshape(N, L, H, DH)

    qs, ks, vs = to_sub(q), to_sub(k), to_sub(v)
    qb = jnp.pad(qs, ((0, 0), (0, Lp - L), (0, 0), (0, 0))).reshape(N, nb, blk, H, DH)
    pad_kv = ((0, 0), (blk, Lp - L + blk), (0, 0), (0, 0))
    kp = jnp.pad(ks, pad_kv).reshape(N, nb + 2, blk, H, DH)
    vp = jnp.pad(vs, pad_kv).reshape(N, nb + 2, blk, H, DH)
    kb = jnp.concatenate([kp[:, :-2], kp[:, 1:-1], kp[:, 2:]], axis=2)
    vb = jnp.concatenate([vp[:, :-2], vp[:, 1:-1], vp[:, 2:]], axis=2)

    scores = jnp.einsum('nbqhd,nbkhd->nbhqk', qb, kb)
    a = jnp.arange(blk)[:, None]
    c = jnp.arange(3 * blk)[None, :]
    rel = c - blk - a
    key_pos = jnp.arange(nb)[:, None, None] * blk - blk + c[None]
    valid = (jnp.abs(rel)[None] <= radius) & (key_pos >= 0) & (key_pos < L)
    dist = (jnp.abs(rel) * dilation).astype(jnp.float32)
    bias = -slopes[:, None, None] * dist[None]
    scores = jnp.where(valid[None, :, None], scores + bias[None, None], MASK_VALUE)

    m = jnp.max(scores, axis=-1, keepdims=True)
    p = jnp.exp(scores - m)
    s = jnp.sum(p, axis=-1, keepdims=True)
    o = jnp.einsum('nbhqk,nbkhd->nbqhd', p / s, vb)
    lse = (m + jnp.log(s))[..., 0].transpose(0, 1, 3, 2)

    o = o.reshape(N, Lp, H, DH)[:, :L].reshape(B, dilation, L, H, DH).transpose(0, 2, 1, 3, 4).reshape(B, S, H, DH)
    lse = lse.reshape(N, Lp, H)[:, :L].reshape(B, dilation, L, H).transpose(0, 2, 1, 3).reshape(B, S, H)
    return o, lse


def setup_inputs(seed: int = 0) -> dict:
    key = jax.random.key(seed)
    ks = jax.random.split(key, 7)
    x = jax.random.normal(ks[0], (BATCH, SEQ, D_MODEL), jnp.float32)
    norm_w = 1.0 + 0.02 * jax.random.normal(ks[1], (D_MODEL,), jnp.float32)
    w_in = jax.random.normal(ks[2], (D_MODEL, IN_WIDTH), jnp.float32) * D_MODEL ** -0.5
    q_norm_w = 1.0 + 0.02 * jax.random.normal(ks[3], (N_ATTN_HEADS, HEAD_DIM), jnp.float32)
    k_norm_w = 1.0 + 0.02 * jax.random.normal(ks[4], (N_ATTN_HEADS, HEAD_DIM), jnp.float32)
    w_fourier = jax.random.normal(ks[5], (N_FOURIER_GROUPS, HEAD_DIM, HEAD_DIM), jnp.float32) * HEAD_DIM ** -0.5
    w_out = jax.random.normal(ks[6], (MIX_WIDTH, D_MODEL), jnp.float32) * MIX_WIDTH ** -0.5
    return {"x": x, "norm_w": norm_w, "w_in": w_in, "q_norm_w": q_norm_w,
            "k_norm_w": k_norm_w, "w_fourier": w_fourier, "w_out": w_out}


def reference(x, norm_w, w_in, q_norm_w, k_norm_w, w_fourier, w_out):
    B, S, _ = x.shape
    slopes = alibi_slopes(N_SLOTS)
    h = x.astype(jnp.float32)
    for _layer in range(DEPTH):
        hn = rms_norm(h, norm_w)
        proj = hn @ w_in.astype(jnp.float32)
        u_f, g_f, q, k, v, g_a = jnp.split(proj, SPLIT_POINTS, axis=-1)

        u_f = u_f.reshape(B, S, N_FOURIER_GROUPS, HEAD_DIM)
        f = jnp.fft.fft2(u_f, axes=(1, 3), norm="ortho").real.astype(jnp.float32)
        f = jnp.einsum('bsgc,gcd->bsgd', f, w_fourier.astype(jnp.float32)).reshape(B, S, FOURIER_WIDTH)
        y_f = f * jax.nn.silu(g_f)

        q = rms_norm(q.reshape(B, S, N_ATTN_HEADS, HEAD_DIM), q_norm_w) * HEAD_DIM ** -0.5
        k = rms_norm(k.reshape(B, S, N_ATTN_HEADS, HEAD_DIM), k_norm_w)
        v = v.reshape(B, S, N_ATTN_HEADS, HEAD_DIM)
        outs, lses = [], []
        for c, (window, dilation) in enumerate(DILATED_CONFIGS):
            sl = slice(c * N_SLOTS, (c + 1) * N_SLOTS)
            o_c, lse_c = dilated_window_attention(q[:, :, sl], k[:, :, sl], v[:, :, sl], window, dilation, slopes)
            outs.append(o_c)
            lses.append(lse_c)
        alpha = jax.nn.softmax(jnp.stack(lses, axis=0), axis=0)
        o = jnp.sum(alpha[..., None] * jnp.stack(outs, axis=0), axis=0).reshape(B, S, ATTN_WIDTH)
        y_a = o * jax.nn.silu(g_a)

        mixed = jnp.concatenate([y_f, y_a], axis=-1) @ w_out.astype(jnp.float32)
        h = h + mixed
    return h.astype(x.dtype)
```

```python
import functools

import numpy as np
import jax
import jax.numpy as jnp
from jax import lax
from jax.experimental import pallas as pl
from jax.experimental.pallas import tpu as pltpu

D_MODEL = 1024
HEAD_DIM = 64
N_SLOTS = 8
GROUP_W = N_SLOTS * HEAD_DIM
DILATIONS = (1, 4, 16)
RADIUS = 64
NORM_EPS = 1e-6
MASK_VALUE = -1e30
SEQ = 8192

NQ, NP, NJ = 16, 32, 16

VMEM_LIMIT = 60 * 1024 * 1024

BF16 = jnp.bfloat16
F32 = jnp.float32


@functools.lru_cache(maxsize=None)
def _dft_stage_matrices():
    q = np.arange(NQ, dtype=np.float64)
    p = np.arange(NP, dtype=np.float64)
    j = np.arange(NJ, dtype=np.float64)
    eye_j = (j[:, None] == j[None, :]).astype(np.float64)

    kq = q
    base = (q[None, None, :, None] * kq[:, None, None, None] / NQ
            + j[None, :, None, None] * kq[:, None, None, None] / (NQ * NP * NJ))
    g1 = np.zeros((NP, 2, NQ, NJ, NQ, NJ))
    for pv in range(NP):
        th = 2 * np.pi * np.broadcast_to(base + pv * kq[:, None, None, None] / (NP * NJ), (NQ, NJ, NQ, NJ))
        d = eye_j[None, :, None, :]
        g1[pv, 0] = np.cos(th) * d
        g1[pv, 1] = -np.sin(th) * d
    g1 = g1.reshape(NP, 2 * NQ * NJ, NQ * NJ)

    kp = p
    ph = 2 * np.pi * np.broadcast_to(p[None, None, :, None] * kp[:, None, None, None] / NP
                                     + j[None, :, None, None] * kp[:, None, None, None] / (NP * NJ),
                                     (NP, NJ, NP, NJ))
    c = np.cos(ph) * eye_j[None, :, None, :]
    s = np.sin(ph) * eye_j[None, :, None, :]
    g2 = np.zeros((NP, 2, NJ, NP, 2, NJ))
    g2[:, 0, :, :, 0, :] = c
    g2[:, 0, :, :, 1, :] = s
    g2[:, 1, :, :, 0, :] = -s
    g2[:, 1, :, :, 1, :] = c
    g2 = g2.reshape(2 * NP * NJ, 2 * NP * NJ)

    kj = j
    eye_q = (q[:, None] == q[None, :]).astype(np.float64)
    ps = 2 * np.pi * np.broadcast_to(j[None, None, None, :] * kj[:, None, None, None] / NJ, (NJ, NQ, NQ, NJ))
    c = np.cos(ps) * eye_q[None, :, :, None]
    s = np.sin(ps) * eye_q[None, :, :, None]
    g3 = np.zeros((2, NJ, NQ, 2, NQ, NJ))
    g3[0, :, :, 0] = c
    g3[0, :, :, 1] = s
    g3[1, :, :, 0] = -s
    g3[1, :, :, 1] = c
    g3 = g3.reshape(2 * NJ * NQ, 2 * NQ * NJ)
    return tuple(np.asarray(g, dtype=np.float32) for g in (g1, g2, g3))


@functools.lru_cache(maxsize=None)
def _head_dim_dft():
    m = np.arange(HEAD_DIM, dtype=np.float64)
    ang = 2 * np.pi * m[:, None] * m[None, :] / HEAD_DIM
    scale = 1.0 / np.sqrt(SEQ * HEAD_DIM)
    return (np.asarray(np.cos(ang) * scale, dtype=np.float32), np.asarray(np.sin(ang) * scale, dtype=np.float32))


@functools.lru_cache(maxsize=None)
def _attn_bias(dilation):
    slopes = 2.0 ** (-8.0 * np.arange(1, N_SLOTS + 1, dtype=np.float64) / N_SLOTS)
    a = np.arange(2 * RADIUS)[:, None]
    c = np.arange(4 * RADIUS)[None, :]
    rel = np.abs(c - RADIUS - a)
    tab = np.zeros((N_SLOTS, 2 * RADIUS, 4 * RADIUS))
    for s in range(N_SLOTS):
        tab[s] = np.where(rel <= RADIUS, -slopes[s] * rel * dilation, MASK_VALUE)
    return np.asarray(tab.reshape(N_SLOTS // 2, 4 * RADIUS, 4 * RADIUS), dtype=np.float32)


@functools.lru_cache(maxsize=None)
def _group_mean_matrix():
    i = np.arange(256)
    return np.asarray(((i[:, None] // HEAD_DIM) == (i[None, :] // HEAD_DIM)) / HEAD_DIM, dtype=np.float32)


@functools.lru_cache(maxsize=None)
def _slot_expand_matrix():
    r = np.arange(128)[:, None]
    c = np.arange(GROUP_W)[None, :]
    return np.asarray((r == c // HEAD_DIM), dtype=np.float32)


@functools.lru_cache(maxsize=None)
def _unpermute_matrix(tm, dilation):
    t = np.arange(tm)
    src = (t % dilation) * (tm // dilation) + t // dilation
    return np.asarray(src[:, None] == np.arange(tm)[None, :], dtype=np.float32)


PROJ_TM = 512
COL_UF, COL_GF, COL_Q, COL_K, COL_V, COL_GA = 0, 1, 2, 5, 8, 11


def _proj_kernel(x_ref, nw_ref, w_ref, gain_ref, e_ref, uf_ref, gf_ref, ga_ref, *rest):
    qkv_refs, hn_ref = rest[:-1], rest[-1]
    x = x_ref[0]
    ms = jnp.mean(x * x, axis=-1, keepdims=True)
    hn32 = x * lax.rsqrt(ms + NORM_EPS) * nw_ref[...]
    n_slab = hn_ref.shape[0]
    for s in range(n_slab):
        hn_ref[s] = hn32[:, s * 128:(s + 1) * 128]
    e = e_ref[...]

    def project(hn, col, gain_row):
        acc = jnp.dot(hn, w_ref[:, col * GROUP_W:(col + 1) * GROUP_W], preferred_element_type=F32)
        if gain_row is not None:
            z = (acc * acc).astype(BF16)
            msq = jnp.concatenate(
                [jnp.dot(z[:, h * 256:(h + 1) * 256], e, preferred_element_type=F32) for h in range(2)], axis=-1)
            acc = acc * lax.rsqrt(msq + NORM_EPS) * gain_ref[gain_row:gain_row + 1, :]
        return acc.astype(BF16)

    hn = hn32.astype(BF16)
    uf_ref[0] = project(hn, COL_UF, None)
    gf_ref[0] = project(hn, COL_GF, None)
    ga_ref[0] = project(hn, COL_GA, None)
    for c, d in enumerate(DILATIONS):
        q_ref, k_ref, v_ref = qkv_refs[3 * c:3 * c + 3]
        rows = PROJ_TM // d
        if d > 1:
            hn = jnp.concatenate(
                [jnp.concatenate([hn_ref[s, pl.ds(r, rows, stride=d), :] for s in range(n_slab)], axis=1)
                 for r in range(d)], axis=0).astype(BF16)
        for o_ref, col, gain_row in ((q_ref, COL_Q + c, c), (k_ref, COL_K + c, 3 + c), (v_ref, COL_V + c, None)):
            res = project(hn, col, gain_row)
            for r in range(d):
                o_ref[0, r] = res[r * rows:(r + 1) * rows]


def _project(x, norm_w, w_in, gains):
    B, S, D = x.shape
    tm = PROJ_TM
    nat = jax.ShapeDtypeStruct((B, S, GROUP_W), BF16)
    nat_spec = pl.BlockSpec((1, tm, GROUP_W), lambda b, i: (b, i, 0))
    out_shape, out_specs = [nat] * 3, [nat_spec] * 3
    for d in DILATIONS:
        out_shape += [jax.ShapeDtypeStruct((B, d, S // d, GROUP_W), BF16)] * 3
        out_specs += [pl.BlockSpec((1, d, tm // d, GROUP_W), lambda b, i: (b, 0, i, 0))] * 3
    const = lambda shp: pl.BlockSpec(shp, lambda b, i: (0,) * len(shp), pipeline_mode=pl.Buffered(1))
    return pl.pallas_call(
        _proj_kernel,
        out_shape=out_shape,
        grid=(B, S // tm),
        in_specs=[pl.BlockSpec((1, tm, D), lambda b, i: (b, i, 0)),
                  const((1, D)), const(w_in.shape), const(gains.shape), const((256, 256))],
        out_specs=out_specs,
        scratch_shapes=[pltpu.VMEM((D // 128, tm, 128), F32)],
        compiler_params=pltpu.CompilerParams(
            dimension_semantics=("arbitrary", "arbitrary"), vmem_limit_bytes=VMEM_LIMIT),
        name="proj",
    )(x, norm_w.reshape(1, D), w_in, gains, jnp.asarray(_group_mean_matrix()).astype(BF16))


ATT_TI = 512
ATT_SUB = 2 * RADIUS


def _attn_kernel(q_ref, kp_ref, kc_ref, kn_ref, vp_ref, vc_ref, vn_ref, bias_ref, o_ref, lse_ref, kbuf, vbuf):
    i = pl.program_id(1)
    first = i == 0
    last = i == pl.num_programs(1) - 1
    kbuf[0:RADIUS] = kp_ref[0]
    kbuf[RADIUS:RADIUS + ATT_TI] = kc_ref[0]
    kbuf[RADIUS + ATT_TI:] = kn_ref[0]
    vbuf[0:RADIUS] = vp_ref[0]
    vbuf[RADIUS:RADIUS + ATT_TI] = vc_ref[0]
    vbuf[RADIUS + ATT_TI:] = vn_ref[0]

    lane = lax.broadcasted_iota(jnp.int32, (ATT_SUB, 128), 1)
    low = lane < HEAD_DIM
    col = lax.broadcasted_iota(jnp.int32, (1, 2 * ATT_SUB), 1)
    n_sub = ATT_TI // ATT_SUB
    for j in range(n_sub):
        r0 = j * ATT_SUB
        pen = jnp.zeros((1, 2 * ATT_SUB), F32)
        if j == 0:
            pen = jnp.where(jnp.logical_and(first, col < RADIUS), MASK_VALUE, pen)
        if j == n_sub - 1:
            pen = jnp.where(jnp.logical_and(last, col >= ATT_SUB + RADIUS), MASK_VALUE, pen)
        lse_acc = jnp.zeros((ATT_SUB, 128), F32)
        for p in range(N_SLOTS // 2):
            c0 = p * 128
            qp = q_ref[0, r0:r0 + ATT_SUB, c0:c0 + 128]
            kw = kbuf[r0:r0 + 2 * ATT_SUB, c0:c0 + 128]
            vw = vbuf[r0:r0 + 2 * ATT_SUB, c0:c0 + 128]
            zero = jnp.zeros_like(qp)
            qs = jnp.concatenate([jnp.where(low, qp, zero), jnp.where(low, zero, qp)], axis=0)
            s = lax.dot_general(qs, kw, (((1,), (1,)), ((), ())), preferred_element_type=F32)
            s = s + bias_ref[p] + pen
            m = jnp.max(s, axis=-1, keepdims=True)
            e = jnp.exp(s - m)
            l = jnp.sum(e, axis=-1, keepdims=True)
            o = jnp.dot(e.astype(BF16), vw, preferred_element_type=F32)
            o = o / l
            o_ref[0, r0:r0 + ATT_SUB, c0:c0 + 128] = jnp.where(low, o[:ATT_SUB], o[ATT_SUB:]).astype(BF16)
            lse = m + jnp.log(l)
            lse_acc = jnp.where(lane == 2 * p, lse[:ATT_SUB], lse_acc)
            lse_acc = jnp.where(lane == 2 * p + 1, lse[ATT_SUB:], lse_acc)
        lse_ref[0, r0:r0 + ATT_SUB, :] = lse_acc


def _attention(q, k, v, dilation):
    N, L, W = q.shape
    nblk = L // ATT_TI
    halo_per_blk = ATT_TI // RADIUS
    n_halo = L // RADIUS
    cur = pl.BlockSpec((1, ATT_TI, W), lambda n, i: (n, i, 0))
    prev = pl.BlockSpec((1, RADIUS, W), lambda n, i: (n, jnp.maximum(i * halo_per_blk - 1, 0), 0))
    nxt = pl.BlockSpec((1, RADIUS, W), lambda n, i: (n, jnp.minimum((i + 1) * halo_per_blk, n_halo - 1), 0))
    return pl.pallas_call(
        _attn_kernel,
        out_shape=[jax.ShapeDtypeStruct((N, L, W), BF16), jax.ShapeDtypeStruct((N, L, 128), F32)],
        grid=(N, nblk),
        in_specs=[cur, prev, cur, nxt, prev, cur, nxt,
                  pl.BlockSpec((N_SLOTS // 2, 2 * ATT_SUB, 2 * ATT_SUB), lambda n, i: (0, 0, 0))],
        out_specs=[pl.BlockSpec((1, ATT_TI, W), lambda n, i: (n, i, 0)),
                   pl.BlockSpec((1, ATT_TI, 128), lambda n, i: (n, i, 0))],
        scratch_shapes=[pltpu.VMEM((ATT_TI + 2 * RADIUS, W), BF16), pltpu.VMEM((ATT_TI + 2 * RADIUS, W), BF16)],
        compiler_params=pltpu.CompilerParams(
            dimension_semantics=("arbitrary", "arbitrary"), vmem_limit_bytes=VMEM_LIMIT),
        name=f"attn_d{dilation}",
    )(q, k, k, k, v, v, v, jnp.asarray(_attn_bias(dilation)))


FFT_CW = 256


def _fft_kernel(u_ref, g1_ref, g2_ref, g3_ref, zr_ref, zi_ref, y_ref):
    cw = u_ref.shape[-1]

    def stage1(p, carry):
        xg = u_ref[0, :, pl.ds(p, 1), :, :].reshape(NQ * NJ, cw)
        r = jnp.dot(g1_ref[p], xg, preferred_element_type=F32)
        y_ref[pl.ds(p, 1)] = r.astype(BF16).reshape(1, 2, NQ, NJ, cw)
        return carry

    lax.fori_loop(0, NP, stage1, 0)

    def stage2(kq, carry):
        xg = y_ref[:, :, pl.ds(kq, 1), :, :].reshape(NP * 2 * NJ, cw)
        r = jnp.dot(g2_ref[...], xg, preferred_element_type=F32)
        y_ref[:, :, pl.ds(kq, 1), :, :] = r.astype(BF16).reshape(NP, 2, 1, NJ, cw)
        return carry

    lax.fori_loop(0, NQ, stage2, 0)

    def stage3(kp, carry):
        xg = y_ref[pl.ds(kp, 1)].reshape(2 * NQ * NJ, cw)
        r = jnp.dot(g3_ref[...], xg, preferred_element_type=F32).astype(BF16)
        half = NJ * NQ
        zr_ref[0, :, pl.ds(kp, 1), :, :] = r[:half].reshape(NJ, 1, NQ, cw)
        zi_ref[0, :, pl.ds(kp, 1), :, :] = r[half:].reshape(NJ, 1, NQ, cw)
        return carry

    lax.fori_loop(0, NP, stage3, 0)


def _seq_dft(u):
    B, S, W = u.shape
    g1, g2, g3 = (jnp.asarray(g).astype(BF16) for g in _dft_stage_matrices())
    u5 = u.reshape(B, NQ, NP, NJ, W)
    blk = pl.BlockSpec((1, NQ, NP, NJ, FFT_CW), lambda b, c: (b, 0, 0, 0, c))
    zr, zi = pl.pallas_call(
        _fft_kernel,
        out_shape=[jax.ShapeDtypeStruct((B, NJ, NP, NQ, W), BF16)] * 2,
        grid=(B, W // FFT_CW),
        in_specs=[blk,
                  pl.BlockSpec(g1.shape, lambda b, c: (0, 0, 0), pipeline_mode=pl.Buffered(1)),
                  pl.BlockSpec(g2.shape, lambda b, c: (0, 0), pipeline_mode=pl.Buffered(1)),
                  pl.BlockSpec(g3.shape, lambda b, c: (0, 0), pipeline_mode=pl.Buffered(1))],
        out_specs=[blk, blk],
        scratch_shapes=[pltpu.VMEM((NP, 2, NQ, NJ, FFT_CW), BF16)],
        compiler_params=pltpu.CompilerParams(
            dimension_semantics=("arbitrary", "arbitrary"), vmem_limit_bytes=VMEM_LIMIT),
        name="seq_dft",
    )(u5, g1, g2, g3)
    return zr.reshape(B, S, W), zi.reshape(B, S, W)


def _fold_kernel(cd_ref, sd_ref, wf_ref, a_ref, b_ref):
    for h in range(2):
        a_ref[h] = jnp.zeros(a_ref.shape[1:], a_ref.dtype)
        b_ref[h] = jnp.zeros(b_ref.shape[1:], b_ref.dtype)
    for g in range(N_SLOTS):
        h, o = divmod(g, 4)
        w = wf_ref[g]
        a = jnp.dot(cd_ref[...], w, preferred_element_type=F32, precision=lax.Precision.HIGHEST)
        b = jnp.dot(sd_ref[...], w, preferred_element_type=F32, precision=lax.Precision.HIGHEST)
        a_ref[h, o * HEAD_DIM:(o + 1) * HEAD_DIM, o * HEAD_DIM:(o + 1) * HEAD_DIM] = a.astype(BF16)
        b_ref[h, o * HEAD_DIM:(o + 1) * HEAD_DIM, o * HEAD_DIM:(o + 1) * HEAD_DIM] = b.astype(BF16)


def _fold_fourier_weights(w_fourier):
    cd, sd = _head_dim_dft()
    shp = jax.ShapeDtypeStruct((2, 256, 256), BF16)
    return pl.pallas_call(_fold_kernel, out_shape=[shp, shp], name="fold_fourier")(
        jnp.asarray(cd), jnp.asarray(sd), w_fourier.astype(F32))


FINAL_TM = 256


def _final_kernel(x_ref, zr_ref, zi_ref, gf_ref, ga_ref, o0_ref, o1_ref, o2_ref, l0_ref, l1_ref, l2_ref,
                  a_ref, b_ref, ex_ref, p1_ref, p2_ref, wo_ref, y_ref, l1_nat, l2_nat):
    tm = FINAL_TM
    fs = []
    for h in range(2):
        sl = slice(h * 256, (h + 1) * 256)
        fs.append(jnp.dot(zr_ref[0, :, sl], a_ref[h], preferred_element_type=F32)
                  + jnp.dot(zi_ref[0, :, sl], b_ref[h], preferred_element_type=F32))
    f = jnp.concatenate(fs, axis=-1)
    gf = gf_ref[0].astype(F32)
    yf = (f * (gf * jax.nn.sigmoid(gf))).astype(BF16)

    for l_ref, l_nat, d in ((l1_ref, l1_nat, DILATIONS[1]), (l2_ref, l2_nat, DILATIONS[2])):
        for r in range(d):
            l_nat[pl.ds(r, tm // d, stride=d), :] = l_ref[0, r]
    o0 = o0_ref[0].astype(F32)
    o1 = jnp.dot(p1_ref[...], o1_ref[0].reshape(tm, GROUP_W), preferred_element_type=F32)
    o2 = jnp.dot(p2_ref[...], o2_ref[0].reshape(tm, GROUP_W), preferred_element_type=F32)

    l0, l1, l2 = l0_ref[0], l1_nat[...], l2_nat[...]
    mx = jnp.maximum(jnp.maximum(l0, l1), l2)
    e0, e1, e2 = jnp.exp(l0 - mx), jnp.exp(l1 - mx), jnp.exp(l2 - mx)
    inv = 1.0 / (e0 + e1 + e2)
    ex = ex_ref[...]
    o = None
    for e_c, o_c in ((e0, o0), (e1, o1), (e2, o2)):
        alpha = jnp.dot((e_c * inv).astype(BF16), ex, preferred_element_type=F32)
        o = alpha * o_c if o is None else o + alpha * o_c
    ga = ga_ref[0].astype(F32)
    ya = (o * (ga * jax.nn.sigmoid(ga))).astype(BF16)

    mixed = (jnp.dot(yf, wo_ref[0:GROUP_W, :], preferred_element_type=F32)
             + jnp.dot(ya, wo_ref[GROUP_W:, :], preferred_element_type=F32))
    y_ref[0] = x_ref[0] + mixed


def _final(x, zr, zi, gf, ga, o_list, lse_list, a_blk, b_blk, w_out):
    B, S, D = x.shape
    tm = FINAL_TM
    d1, d2 = DILATIONS[1], DILATIONS[2]
    nat = lambda w: pl.BlockSpec((1, tm, w), lambda b, i: (b, i, 0))
    perm = lambda d, w: pl.BlockSpec((1, d, tm // d, w), lambda b, i: (b, 0, i, 0))
    const = lambda shp: pl.BlockSpec(shp, lambda b, i: (0,) * len(shp), pipeline_mode=pl.Buffered(1))
    p1 = jnp.asarray(_unpermute_matrix(tm, d1)).astype(BF16)
    p2 = jnp.asarray(_unpermute_matrix(tm, d2)).astype(BF16)
    ex = jnp.asarray(_slot_expand_matrix()).astype(BF16)
    return pl.pallas_call(
        _final_kernel,
        out_shape=jax.ShapeDtypeStruct((B, S, D), F32),
        grid=(B, S // tm),
        in_specs=[nat(D), nat(GROUP_W), nat(GROUP_W), nat(GROUP_W), nat(GROUP_W),
                  nat(GROUP_W), perm(d1, GROUP_W), perm(d2, GROUP_W), nat(128), perm(d1, 128), perm(d2, 128),
                  const((2, 256, 256)), const((2, 256, 256)), const((128, GROUP_W)),
                  const((tm, tm)), const((tm, tm)), const((D, D))],
        out_specs=nat(D),
        scratch_shapes=[pltpu.VMEM((tm, 128), F32), pltpu.VMEM((tm, 128), F32)],
        compiler_params=pltpu.CompilerParams(
            dimension_semantics=("arbitrary", "arbitrary"), vmem_limit_bytes=VMEM_LIMIT),
        name="final_mix",
    )(x, zr, zi, gf, ga, *o_list, *lse_list, a_blk, b_blk, ex, p1, p2, w_out)


def kernel(x, norm_w, w_in, q_norm_w, k_norm_w, w_fourier, w_out):
    B, S, D = x.shape
    n_cfg = len(DILATIONS)
    xf = x.astype(F32)
    gains = jnp.concatenate([q_norm_w.astype(F32).reshape(n_cfg, GROUP_W) * (HEAD_DIM ** -0.5),
                             k_norm_w.astype(F32).reshape(n_cfg, GROUP_W)], axis=0)
    proj = _project(xf, norm_w.astype(F32), w_in.astype(BF16), gains)
    u_f, g_f, g_a = proj[:3]

    outs, lses = [], []
    for c, d in enumerate(DILATIONS):
        q, k, v = (t.reshape(B * d, S // d, GROUP_W) for t in proj[3 + 3 * c:6 + 3 * c])
        o, lse = _attention(q, k, v, d)
        if d == 1:
            outs.append(o)
            lses.append(lse)
        else:
            outs.append(o.reshape(B, d, S // d, GROUP_W))
            lses.append(lse.reshape(B, d, S // d, 128))

    zr, zi = _seq_dft(u_f)
    a_blk, b_blk = _fold_fourier_weights(w_fourier)
    y = _final(xf, zr, zi, g_f, g_a, outs, lses, a_blk, b_blk, w_out.astype(BF16))
    return y.astype(x.dtype)
```

```python
import functools

import numpy as np
import jax
import jax.numpy as jnp
from jax import lax
from jax.experimental import pallas as pl
from jax.experimental.pallas import tpu as pltpu

D_MODEL = 1024
HEAD_DIM = 64
N_SLOTS = 8
GROUP_W = N_SLOTS * HEAD_DIM
DILATIONS = (1, 4, 16)
RADIUS = 64
NORM_EPS = 1e-6
MASK_VALUE = -1e30
LOG2E = float(np.log2(np.e))
SEQ = 8192

NQ, NP, NJ = 16, 32, 16

VMEM_LIMIT = 60 * 1024 * 1024

BF16 = jnp.bfloat16
F32 = jnp.float32


@functools.lru_cache(maxsize=None)
def _dft_stage_matrices():
    q = np.arange(NQ, dtype=np.float64)
    p = np.arange(NP, dtype=np.float64)
    j = np.arange(NJ, dtype=np.float64)
    eye_j = (j[:, None] == j[None, :]).astype(np.float64)

    kq = q
    base = (q[None, None, :, None] * kq[:, None, None, None] / NQ
            + j[None, :, None, None] * kq[:, None, None, None] / (NQ * NP * NJ))
    g1 = np.zeros((NP, 2, NQ, NJ, NQ, NJ))
    for pv in range(NP):
        th = 2 * np.pi * np.broadcast_to(base + pv * kq[:, None, None, None] / (NP * NJ), (NQ, NJ, NQ, NJ))
        d = eye_j[None, :, None, :]
        g1[pv, 0] = np.cos(th) * d
        g1[pv, 1] = -np.sin(th) * d
    g1 = g1.reshape(NP, 2 * NQ * NJ, NQ * NJ)

    kp = p
    ph = 2 * np.pi * np.broadcast_to(p[None, None, :, None] * kp[:, None, None, None] / NP
                                     + j[None, :, None, None] * kp[:, None, None, None] / (NP * NJ),
                                     (NP, NJ, NP, NJ))
    c = np.cos(ph) * eye_j[None, :, None, :]
    s = np.sin(ph) * eye_j[None, :, None, :]
    g2 = np.zeros((NP, 2, NJ, NP, 2, NJ))
    g2[:, 0, :, :, 0, :] = c
    g2[:, 0, :, :, 1, :] = s
    g2[:, 1, :, :, 0, :] = -s
    g2[:, 1, :, :, 1, :] = c
    g2 = g2.reshape(2 * NP * NJ, 2 * NP * NJ)

    kj = j
    eye_q = (q[:, None] == q[None, :]).astype(np.float64)
    ps = 2 * np.pi * np.broadcast_to(j[None, None, None, :] * kj[:, None, None, None] / NJ, (NJ, NQ, NQ, NJ))
    c = np.cos(ps) * eye_q[None, :, :, None]
    s = np.sin(ps) * eye_q[None, :, :, None]
    g3 = np.zeros((2, NJ, NQ, 2, NQ, NJ))
    g3[0, :, :, 0] = c
    g3[0, :, :, 1] = s
    g3[1, :, :, 0] = -s
    g3[1, :, :, 1] = c
    g3 = g3.reshape(2 * NJ * NQ, 2 * NQ * NJ)
    return tuple(np.asarray(g, dtype=np.float32) for g in (g1, g2, g3))


@functools.lru_cache(maxsize=None)
def _head_dim_dft():
    m = np.arange(HEAD_DIM, dtype=np.float64)
    ang = 2 * np.pi * m[:, None] * m[None, :] / HEAD_DIM
    scale = 1.0 / np.sqrt(SEQ * HEAD_DIM)
    return (np.asarray(np.cos(ang) * scale, dtype=np.float32), np.asarray(np.sin(ang) * scale, dtype=np.float32))


@functools.lru_cache(maxsize=None)
def _attn_bias(dilation):
    slopes = 2.0 ** (-8.0 * np.arange(1, N_SLOTS + 1, dtype=np.float64) / N_SLOTS)
    a = np.arange(2 * RADIUS)[:, None]
    c = np.arange(4 * RADIUS)[None, :]
    rel = np.abs(c - RADIUS - a)
    key_ok = (np.ones_like(c, dtype=bool), c >= RADIUS, c < 3 * RADIUS)
    tab = np.zeros((3, N_SLOTS, 2 * RADIUS, 4 * RADIUS))
    for var in range(3):
        for s in range(N_SLOTS):
            tab[var, s] = np.where((rel <= RADIUS) & key_ok[var], -slopes[s] * rel * dilation * LOG2E, MASK_VALUE)
    return np.asarray(tab.reshape(3, N_SLOTS // 2, 4 * RADIUS, 4 * RADIUS), dtype=np.float32)


@functools.lru_cache(maxsize=None)
def _group_mean_matrix():
    i = np.arange(256)
    return np.asarray(((i[:, None] // HEAD_DIM) == (i[None, :] // HEAD_DIM)) / HEAD_DIM, dtype=np.float32)


@functools.lru_cache(maxsize=None)
def _slot_expand_matrix():
    r = np.arange(128)[:, None]
    c = np.arange(GROUP_W)[None, :]
    return np.asarray((r == c // HEAD_DIM), dtype=np.float32)


@functools.lru_cache(maxsize=None)
def _unpermute_matrix(tm, dilation):
    t = np.arange(tm)
    src = (t % dilation) * (tm // dilation) + t // dilation
    return np.asarray(src[:, None] == np.arange(tm)[None, :], dtype=np.float32)


PROJ_TM = 512
COL_UF, COL_GF, COL_Q, COL_K, COL_V, COL_GA = 0, 1, 2, 5, 8, 11


def _proj_kernel(x_ref, nw_ref, w_ref, gain_ref, e_ref, uf_ref, gf_ref, ga_ref, *rest):
    qkv_refs, hn_ref = rest[:-1], rest[-1]
    x = x_ref[0]
    ms = jnp.mean(x * x, axis=-1, keepdims=True)
    hn32 = x * lax.rsqrt(ms + NORM_EPS) * nw_ref[...]
    n_slab = hn_ref.shape[0]
    for s in range(n_slab):
        hn_ref[s] = hn32[:, s * 128:(s + 1) * 128]
    e = e_ref[...]

    def project(hn, col, gain_row):
        acc = jnp.dot(hn, w_ref[:, col * GROUP_W:(col + 1) * GROUP_W], preferred_element_type=F32)
        if gain_row is not None:
            z = (acc * acc).astype(BF16)
            msq = jnp.concatenate(
                [jnp.dot(z[:, h * 256:(h + 1) * 256], e, preferred_element_type=F32) for h in range(2)], axis=-1)
            acc = acc * lax.rsqrt(msq + NORM_EPS) * gain_ref[gain_row:gain_row + 1, :]
        return acc.astype(BF16)

    hn = hn32.astype(BF16)
    uf_ref[0] = project(hn, COL_UF, None)
    gf_ref[0] = project(hn, COL_GF, None)
    ga_ref[0] = project(hn, COL_GA, None)
    for c, d in enumerate(DILATIONS):
        q_ref, k_ref, v_ref = qkv_refs[3 * c:3 * c + 3]
        rows = PROJ_TM // d
        if d > 1:
            hn = jnp.concatenate(
                [jnp.concatenate([hn_ref[s, pl.ds(r, rows, stride=d), :] for s in range(n_slab)], axis=1)
                 for r in range(d)], axis=0).astype(BF16)
        for o_ref, col, gain_row in ((q_ref, COL_Q + c, c), (k_ref, COL_K + c, 3 + c), (v_ref, COL_V + c, None)):
            res = project(hn, col, gain_row)
            for r in range(d):
                o_ref[0, r] = res[r * rows:(r + 1) * rows]


def _project(x, norm_w, w_in, gains):
    B, S, D = x.shape
    tm = PROJ_TM
    nat = jax.ShapeDtypeStruct((B, S, GROUP_W), BF16)
    nat_spec = pl.BlockSpec((1, tm, GROUP_W), lambda b, i: (b, i, 0))
    out_shape, out_specs = [nat] * 3, [nat_spec] * 3
    for d in DILATIONS:
        out_shape += [jax.ShapeDtypeStruct((B, d, S // d, GROUP_W), BF16)] * 3
        out_specs += [pl.BlockSpec((1, d, tm // d, GROUP_W), lambda b, i: (b, 0, i, 0))] * 3
    const = lambda shp: pl.BlockSpec(shp, lambda b, i: (0,) * len(shp), pipeline_mode=pl.Buffered(1))
    return pl.pallas_call(
        _proj_kernel,
        out_shape=out_shape,
        grid=(B, S // tm),
        in_specs=[pl.BlockSpec((1, tm, D), lambda b, i: (b, i, 0)),
                  const((1, D)), const(w_in.shape), const(gains.shape), const((256, 256))],
        out_specs=out_specs,
        scratch_shapes=[pltpu.VMEM((D // 128, tm, 128), F32)],
        compiler_params=pltpu.CompilerParams(
            dimension_semantics=("arbitrary", "arbitrary"), vmem_limit_bytes=VMEM_LIMIT),
        name="proj",
    )(x, norm_w.reshape(1, D), w_in, gains, jnp.asarray(_group_mean_matrix()).astype(BF16))


ATT_TI = 512
ATT_SUB = 2 * RADIUS


def _attn_kernel(q_ref, kp_ref, kc_ref, kn_ref, vp_ref, vc_ref, vn_ref, bias_ref, o_ref, st_ref, kbuf, vbuf):
    i = pl.program_id(1)
    first = i == 0
    last = i == pl.num_programs(1) - 1
    kbuf[0:RADIUS] = kp_ref[0]
    kbuf[RADIUS:RADIUS + ATT_TI] = kc_ref[0]
    kbuf[RADIUS + ATT_TI:] = kn_ref[0]
    vbuf[0:RADIUS] = vp_ref[0]
    vbuf[RADIUS:RADIUS + ATT_TI] = vc_ref[0]
    vbuf[RADIUS + ATT_TI:] = vn_ref[0]

    lane = lax.broadcasted_iota(jnp.int32, (ATT_SUB, 128), 1)
    low = lane < HEAD_DIM
    ones = jnp.ones((2 * ATT_SUB, 128), BF16)
    n_sub = ATT_TI // ATT_SUB
    for j in range(n_sub):
        r0 = j * ATT_SUB
        variant = 0
        if j == 0:
            variant = jnp.where(first, 1, 0)
        if j == n_sub - 1:
            variant = jnp.where(last, 2, 0)
        st = jnp.zeros((ATT_SUB, 128), F32)
        for p in range(N_SLOTS // 2):
            c0 = p * 128
            qp = q_ref[0, r0:r0 + ATT_SUB, c0:c0 + 128]
            kw = kbuf[r0:r0 + 2 * ATT_SUB, c0:c0 + 128]
            vw = jnp.concatenate([vbuf[r0:r0 + 2 * ATT_SUB, c0:c0 + 128], ones], axis=1)
            zero = jnp.zeros_like(qp)
            qs = jnp.concatenate([jnp.where(low, qp, zero), jnp.where(low, zero, qp)], axis=0)
            s = lax.dot_general(qs, kw, (((1,), (1,)), ((), ())), preferred_element_type=F32)
            s = s + bias_ref[variant, p]
            m = jnp.max(s, axis=-1, keepdims=True)
            e = jnp.exp2(s - m).astype(BF16)
            ol = jnp.dot(e, vw, preferred_element_type=F32)
            o_ref[0, r0:r0 + ATT_SUB, c0:c0 + 128] = jnp.where(
                low, ol[:ATT_SUB, :128], ol[ATT_SUB:, :128]).astype(BF16)
            st = jnp.where(lane == 2 * p, m[:ATT_SUB], st)
            st = jnp.where(lane == 2 * p + 1, m[ATT_SUB:], st)
            st = jnp.where(lane == N_SLOTS + 2 * p, ol[:ATT_SUB, 128:], st)
            st = jnp.where(lane == N_SLOTS + 2 * p + 1, ol[ATT_SUB:, 128:], st)
        st_ref[0, r0:r0 + ATT_SUB, :] = st


def _attention(q, k, v, dilation):
    N, L, W = q.shape
    nblk = L // ATT_TI
    halo_per_blk = ATT_TI // RADIUS
    n_halo = L // RADIUS
    bias = jnp.asarray(_attn_bias(dilation))
    cur = pl.BlockSpec((1, ATT_TI, W), lambda n, i: (n, i, 0))
    prev = pl.BlockSpec((1, RADIUS, W), lambda n, i: (n, jnp.maximum(i * halo_per_blk - 1, 0), 0))
    nxt = pl.BlockSpec((1, RADIUS, W), lambda n, i: (n, jnp.minimum((i + 1) * halo_per_blk, n_halo - 1), 0))
    return pl.pallas_call(
        _attn_kernel,
        out_shape=[jax.ShapeDtypeStruct((N, L, W), BF16), jax.ShapeDtypeStruct((N, L, 128), F32)],
        grid=(N, nblk),
        in_specs=[cur, prev, cur, nxt, prev, cur, nxt,
                  pl.BlockSpec(bias.shape, lambda n, i: (0, 0, 0, 0), pipeline_mode=pl.Buffered(1))],
        out_specs=[pl.BlockSpec((1, ATT_TI, W), lambda n, i: (n, i, 0)),
                   pl.BlockSpec((1, ATT_TI, 128), lambda n, i: (n, i, 0))],
        scratch_shapes=[pltpu.VMEM((ATT_TI + 2 * RADIUS, W), BF16), pltpu.VMEM((ATT_TI + 2 * RADIUS, W), BF16)],
        compiler_params=pltpu.CompilerParams(
            dimension_semantics=("arbitrary", "arbitrary"), vmem_limit_bytes=VMEM_LIMIT),
        name=f"attn_d{dilation}",
    )(q, k, k, k, v, v, v, bias)


FFT_CW = 256
FFT_UNROLL = 4


def _fft_kernel(u_ref, g1_ref, g2_ref, g3_ref, zr_ref, zi_ref, y_ref):
    cw = u_ref.shape[-1]

    def stage1(p, carry):
        xg = u_ref[0, :, pl.ds(p, 1), :, :].reshape(NQ * NJ, cw)
        r = jnp.dot(g1_ref[p], xg, preferred_element_type=F32)
        y_ref[pl.ds(p, 1)] = r.astype(BF16).reshape(1, 2, NQ, NJ, cw)
        return carry

    lax.fori_loop(0, NP, stage1, 0, unroll=FFT_UNROLL)

    def stage2(kq, carry):
        xg = y_ref[:, :, pl.ds(kq, 1), :, :].reshape(NP * 2 * NJ, cw)
        r = jnp.dot(g2_ref[...], xg, preferred_element_type=F32)
        y_ref[:, :, pl.ds(kq, 1), :, :] = r.astype(BF16).reshape(NP, 2, 1, NJ, cw)
        return carry

    lax.fori_loop(0, NQ, stage2, 0, unroll=2)

    def stage3(kp, carry):
        xg = y_ref[pl.ds(kp, 1)].reshape(2 * NQ * NJ, cw)
        r = jnp.dot(g3_ref[...], xg, preferred_element_type=F32).astype(BF16)
        half = NJ * NQ
        zr_ref[0, :, pl.ds(kp, 1), :, :] = r[:half].reshape(NJ, 1, NQ, cw)
        zi_ref[0, :, pl.ds(kp, 1), :, :] = r[half:].reshape(NJ, 1, NQ, cw)
        return carry

    lax.fori_loop(0, NP, stage3, 0, unroll=FFT_UNROLL)


def _seq_dft(u):
    B, S, W = u.shape
    g1, g2, g3 = (jnp.asarray(g).astype(BF16) for g in _dft_stage_matrices())
    u5 = u.reshape(B, NQ, NP, NJ, W)
    blk = pl.BlockSpec((1, NQ, NP, NJ, FFT_CW), lambda b, c: (b, 0, 0, 0, c))
    zr, zi = pl.pallas_call(
        _fft_kernel,
        out_shape=[jax.ShapeDtypeStruct((B, NJ, NP, NQ, W), BF16)] * 2,
        grid=(B, W // FFT_CW),
        in_specs=[blk,
                  pl.BlockSpec(g1.shape, lambda b, c: (0, 0, 0), pipeline_mode=pl.Buffered(1)),
                  pl.BlockSpec(g2.shape, lambda b, c: (0, 0), pipeline_mode=pl.Buffered(1)),
                  pl.BlockSpec(g3.shape, lambda b, c: (0, 0), pipeline_mode=pl.Buffered(1))],
        out_specs=[blk, blk],
        scratch_shapes=[pltpu.VMEM((NP, 2, NQ, NJ, FFT_CW), BF16)],
        compiler_params=pltpu.CompilerParams(
            dimension_semantics=("arbitrary", "arbitrary"), vmem_limit_bytes=VMEM_LIMIT),
        name="seq_dft",
    )(u5, g1, g2, g3)
    return zr.reshape(B, S, W), zi.reshape(B, S, W)


def _fold_kernel(cd_ref, sd_ref, wf_ref, a_ref, b_ref):
    for h in range(2):
        a_ref[h] = jnp.zeros(a_ref.shape[1:], a_ref.dtype)
        b_ref[h] = jnp.zeros(b_ref.shape[1:], b_ref.dtype)
    for g in range(N_SLOTS):
        h, o = divmod(g, 4)
        w = wf_ref[g]
        a = jnp.dot(cd_ref[...], w, preferred_element_type=F32, precision=lax.Precision.HIGHEST)
        b = jnp.dot(sd_ref[...], w, preferred_element_type=F32, precision=lax.Precision.HIGHEST)
        a_ref[h, o * HEAD_DIM:(o + 1) * HEAD_DIM, o * HEAD_DIM:(o + 1) * HEAD_DIM] = a.astype(BF16)
        b_ref[h, o * HEAD_DIM:(o + 1) * HEAD_DIM, o * HEAD_DIM:(o + 1) * HEAD_DIM] = b.astype(BF16)


def _fold_fourier_weights(w_fourier):
    cd, sd = _head_dim_dft()
    shp = jax.ShapeDtypeStruct((2, 256, 256), BF16)
    return pl.pallas_call(_fold_kernel, out_shape=[shp, shp], name="fold_fourier")(
        jnp.asarray(cd), jnp.asarray(sd), w_fourier.astype(F32))


FINAL_TM = 256


def _final_kernel(x_ref, zr_ref, zi_ref, gf_ref, ga_ref, o0_ref, o1_ref, o2_ref, l0_ref, l1_ref, l2_ref,
                  a_ref, b_ref, ex_ref, p1_ref, p2_ref, wo_ref, y_ref, l1_nat, l2_nat):
    tm = FINAL_TM
    fs = []
    for h in range(2):
        sl = slice(h * 256, (h + 1) * 256)
        fs.append(jnp.dot(zr_ref[0, :, sl], a_ref[h], preferred_element_type=F32)
                  + jnp.dot(zi_ref[0, :, sl], b_ref[h], preferred_element_type=F32))
    f = jnp.concatenate(fs, axis=-1)
    gf = gf_ref[0].astype(F32)
    yf = (f * (gf * jax.nn.sigmoid(gf))).astype(BF16)

    for l_ref, l_nat, d in ((l1_ref, l1_nat, DILATIONS[1]), (l2_ref, l2_nat, DILATIONS[2])):
        for r in range(d):
            l_nat[pl.ds(r, tm // d, stride=d), :] = l_ref[0, r]
    o0 = o0_ref[0].astype(F32)
    o1 = jnp.dot(p1_ref[...], o1_ref[0].reshape(tm, GROUP_W), preferred_element_type=F32)
    o2 = jnp.dot(p2_ref[...], o2_ref[0].reshape(tm, GROUP_W), preferred_element_type=F32)

    sts = (l0_ref[0], l1_nat[...], l2_nat[...])
    mx = jnp.maximum(jnp.maximum(sts[0], sts[1]), sts[2])
    es = [jnp.exp2(st - mx) for st in sts]
    den = None
    for e_c, st in zip(es, sts):
        term = e_c * pltpu.roll(st, 128 - N_SLOTS, 1)
        den = term if den is None else den + term
    slot_lane = lax.broadcasted_iota(jnp.int32, den.shape, 1) < N_SLOTS
    inv = 1.0 / jnp.where(slot_lane, den, 1.0)
    ex = ex_ref[...]
    o = None
    for e_c, o_c in zip(es, (o0, o1, o2)):
        alpha = jnp.dot((e_c * inv).astype(BF16), ex, preferred_element_type=F32)
        o = alpha * o_c if o is None else o + alpha * o_c
    ga = ga_ref[0].astype(F32)
    ya = (o * (ga * jax.nn.sigmoid(ga))).astype(BF16)

    mixed = (jnp.dot(yf, wo_ref[0:GROUP_W, :], preferred_element_type=F32)
             + jnp.dot(ya, wo_ref[GROUP_W:, :], preferred_element_type=F32))
    y_ref[0] = x_ref[0] + mixed


def _final(x, zr, zi, gf, ga, o_list, lse_list, a_blk, b_blk, w_out):
    B, S, D = x.shape
    tm = FINAL_TM
    d1, d2 = DILATIONS[1], DILATIONS[2]
    nat = lambda w: pl.BlockSpec((1, tm, w), lambda b, i: (b, i, 0))
    perm = lambda d, w: pl.BlockSpec((1, d, tm // d, w), lambda b, i: (b, 0, i, 0))
    const = lambda shp: pl.BlockSpec(shp, lambda b, i: (0,) * len(shp), pipeline_mode=pl.Buffered(1))
    p1 = jnp.asarray(_unpermute_matrix(tm, d1)).astype(BF16)
    p2 = jnp.asarray(_unpermute_matrix(tm, d2)).astype(BF16)
    ex = jnp.asarray(_slot_expand_matrix()).astype(BF16)
    return pl.pallas_call(
        _final_kernel,
        out_shape=jax.ShapeDtypeStruct((B, S, D), F32),
        grid=(B, S // tm),
        in_specs=[nat(D), nat(GROUP_W), nat(GROUP_W), nat(GROUP_W), nat(GROUP_W),
                  nat(GROUP_W), perm(d1, GROUP_W), perm(d2, GROUP_W), nat(128), perm(d1, 128), perm(d2, 128),
                  const((2, 256, 256)), const((2, 256, 256)), const((128, GROUP_W)),
                  const((tm, tm)), const((tm, tm)), const((D, D))],
        out_specs=nat(D),
        scratch_shapes=[pltpu.VMEM((tm, 128), F32), pltpu.VMEM((tm, 128), F32)],
        compiler_params=pltpu.CompilerParams(
            dimension_semantics=("arbitrary", "arbitrary"), vmem_limit_bytes=VMEM_LIMIT),
        name="final_mix",
    )(x, zr, zi, gf, ga, *o_list, *lse_list, a_blk, b_blk, ex, p1, p2, w_out)


def kernel(x, norm_w, w_in, q_norm_w, k_norm_w, w_fourier, w_out):
    B, S, D = x.shape
    n_cfg = len(DILATIONS)
    xf = x.astype(F32)
    gains = jnp.concatenate([q_norm_w.astype(F32).reshape(n_cfg, GROUP_W) * (HEAD_DIM ** -0.5 * LOG2E),
                             k_norm_w.astype(F32).reshape(n_cfg, GROUP_W)], axis=0)
    proj = _project(xf, norm_w.astype(F32), w_in.astype(BF16), gains)
    u_f, g_f, g_a = proj[:3]

    outs, lses = [], []
    for c, d in enumerate(DILATIONS):
        q, k, v = (t.reshape(B * d, S // d, GROUP_W) for t in proj[3 + 3 * c:6 + 3 * c])
        o, lse = _attention(q, k, v, d)
        if d == 1:
            outs.append(o)
            lses.append(lse)
        else:
            outs.append(o.reshape(B, d, S // d, GROUP_W))
            lses.append(lse.reshape(B, d, S // d, 128))

    zr, zi = _seq_dft(u_f)
    a_blk, b_blk = _fold_fourier_weights(w_fourier)
    y = _final(xf, zr, zi, g_f, g_a, outs, lses, a_blk, b_blk, w_out.astype(BF16))
    return y.astype(x.dtype)
```

```python
import functools

import numpy as np
import jax
import jax.numpy as jnp
from jax import lax
from jax.experimental import pallas as pl
from jax.experimental.pallas import tpu as pltpu

D_MODEL = 1024
HEAD_DIM = 64
N_SLOTS = 8
GROUP_W = N_SLOTS * HEAD_DIM
DILATIONS = (1, 4, 16)
RADIUS = 64
NORM_EPS = 1e-6
MASK_VALUE = -1e30
LOG2E = float(np.log2(np.e))
SEQ = 8192

NQ, NP, NJ = 16, 32, 16

VMEM_LIMIT = 60 * 1024 * 1024

BF16 = jnp.bfloat16
F32 = jnp.float32


@functools.lru_cache(maxsize=None)
def _dft_stage_matrices():
    q = np.arange(NQ, dtype=np.float64)
    p = np.arange(NP, dtype=np.float64)
    j = np.arange(NJ, dtype=np.float64)
    eye_j = (j[:, None] == j[None, :]).astype(np.float64)

    kq = q
    base = (q[None, None, :, None] * kq[:, None, None, None] / NQ
            + j[None, :, None, None] * kq[:, None, None, None] / (NQ * NP * NJ))
    g1 = np.zeros((NP, 2, NQ, NJ, NQ, NJ))
    for pv in range(NP):
        th = 2 * np.pi * np.broadcast_to(base + pv * kq[:, None, None, None] / (NP * NJ), (NQ, NJ, NQ, NJ))
        d = eye_j[None, :, None, :]
        g1[pv, 0] = np.cos(th) * d
        g1[pv, 1] = -np.sin(th) * d
    g1 = g1.reshape(NP, 2 * NQ * NJ, NQ * NJ)

    kp = p
    ph = 2 * np.pi * np.broadcast_to(p[None, None, :, None] * kp[:, None, None, None] / NP
                                     + j[None, :, None, None] * kp[:, None, None, None] / (NP * NJ),
                                     (NP, NJ, NP, NJ))
    c = np.cos(ph) * eye_j[None, :, None, :]
    s = np.sin(ph) * eye_j[None, :, None, :]
    g2 = np.zeros((NP, 2, NJ, NP, 2, NJ))
    g2[:, 0, :, :, 0, :] = c
    g2[:, 0, :, :, 1, :] = s
    g2[:, 1, :, :, 0, :] = -s
    g2[:, 1, :, :, 1, :] = c
    g2 = g2.reshape(2 * NP * NJ, 2 * NP * NJ)

    kj = j
    eye_q = (q[:, None] == q[None, :]).astype(np.float64)
    ps = 2 * np.pi * np.broadcast_to(j[None, None, None, :] * kj[:, None, None, None] / NJ, (NJ, NQ, NQ, NJ))
    c = np.cos(ps) * eye_q[None, :, :, None]
    s = np.sin(ps) * eye_q[None, :, :, None]
    g3 = np.zeros((2, NJ, NQ, 2, NQ, NJ))
    g3[0, :, :, 0] = c
    g3[0, :, :, 1] = s
    g3[1, :, :, 0] = -s
    g3[1, :, :, 1] = c
    g3 = g3.reshape(2 * NJ * NQ, 2 * NQ * NJ)
    return tuple(np.asarray(g, dtype=np.float32) for g in (g1, g2, g3))


@functools.lru_cache(maxsize=None)
def _head_dim_dft():
    m = np.arange(HEAD_DIM, dtype=np.float64)
    ang = 2 * np.pi * m[:, None] * m[None, :] / HEAD_DIM
    scale = 1.0 / np.sqrt(SEQ * HEAD_DIM)
    return (np.asarray(np.cos(ang) * scale, dtype=np.float32), np.asarray(np.sin(ang) * scale, dtype=np.float32))


@functools.lru_cache(maxsize=None)
def _attn_bias(dilation):
    slopes = 2.0 ** (-8.0 * np.arange(1, N_SLOTS + 1, dtype=np.float64) / N_SLOTS)
    a = np.arange(2 * RADIUS)[:, None]
    c = np.arange(4 * RADIUS)[None, :]
    rel = np.abs(c - RADIUS - a)
    key_ok = (np.ones_like(c, dtype=bool), c >= RADIUS, c < 3 * RADIUS)
    tab = np.zeros((3, N_SLOTS, 2 * RADIUS, 4 * RADIUS))
    for var in range(3):
        for s in range(N_SLOTS):
            tab[var, s] = np.where((rel <= RADIUS) & key_ok[var], -slopes[s] * rel * dilation * LOG2E, MASK_VALUE)
    return np.asarray(tab.reshape(3, N_SLOTS // 2, 4 * RADIUS, 4 * RADIUS), dtype=np.float32)


@functools.lru_cache(maxsize=None)
def _slot_expand_matrix():
    r = np.arange(128)[:, None]
    c = np.arange(GROUP_W)[None, :]
    return np.asarray((r == c // HEAD_DIM), dtype=np.float32)


@functools.lru_cache(maxsize=None)
def _unpermute_matrix(tm, dilation):
    t = np.arange(tm)
    src = (t % dilation) * (tm // dilation) + t // dilation
    return np.asarray(src[:, None] == np.arange(tm)[None, :], dtype=np.float32)


PROJ_TM = 512
COL_UF, COL_GF, COL_Q, COL_K, COL_V, COL_GA = 0, 1, 2, 5, 8, 11


def _proj_kernel(x_ref, nw_ref, w_ref, gain_ref, uf_ref, gf_ref, ga_ref, *rest):
    qkv_refs, hn_ref = rest[:-1], rest[-1]
    x = x_ref[0]
    ms = jnp.mean(x * x, axis=-1, keepdims=True)
    hn32 = x * lax.rsqrt(ms + NORM_EPS) * nw_ref[...]
    n_slab = hn_ref.shape[0]
    for s in range(n_slab):
        hn_ref[s] = hn32[:, s * 128:(s + 1) * 128]
    low = lax.broadcasted_iota(jnp.int32, (PROJ_TM, 128), 1) < HEAD_DIM

    def project(hn, col, gain_row):
        acc = jnp.dot(hn, w_ref[:, col * GROUP_W:(col + 1) * GROUP_W], preferred_element_type=F32)
        if gain_row is None:
            return acc.astype(BF16)
        tiles = []
        for t in range(GROUP_W // 128):
            a = acc[:, t * 128:(t + 1) * 128]
            z = a * a
            z_lo = jnp.where(low, z, 0.0)
            ms_lo = jnp.sum(z_lo, axis=-1, keepdims=True) * (1.0 / HEAD_DIM)
            ms_hi = jnp.sum(z - z_lo, axis=-1, keepdims=True) * (1.0 / HEAD_DIM)
            scale = jnp.where(low, lax.rsqrt(ms_lo + NORM_EPS), lax.rsqrt(ms_hi + NORM_EPS))
            tiles.append(a * scale)
        return (jnp.concatenate(tiles, axis=-1) * gain_ref[gain_row:gain_row + 1, :]).astype(BF16)

    hn = hn32.astype(BF16)
    uf_ref[0] = project(hn, COL_UF, None)
    gf_ref[0] = project(hn, COL_GF, None)
    ga_ref[0] = project(hn, COL_GA, None)
    for c, d in enumerate(DILATIONS):
        q_ref, k_ref, v_ref = qkv_refs[3 * c:3 * c + 3]
        rows = PROJ_TM // d
        if d > 1:
            hn = jnp.concatenate(
                [jnp.concatenate([hn_ref[s, pl.ds(r, rows, stride=d), :] for s in range(n_slab)], axis=1)
                 for r in range(d)], axis=0).astype(BF16)
        for o_ref, col, gain_row in ((q_ref, COL_Q + c, c), (k_ref, COL_K + c, 3 + c), (v_ref, COL_V + c, None)):
            res = project(hn, col, gain_row)
            for r in range(d):
                o_ref[0, r] = res[r * rows:(r + 1) * rows]


def _project(x, norm_w, w_in, gains):
    B, S, D = x.shape
    tm = PROJ_TM
    nat = jax.ShapeDtypeStruct((B, S, GROUP_W), BF16)
    nat_spec = pl.BlockSpec((1, tm, GROUP_W), lambda b, i: (b, i, 0))
    out_shape, out_specs = [nat] * 3, [nat_spec] * 3
    for d in DILATIONS:
        out_shape += [jax.ShapeDtypeStruct((B, d, S // d, GROUP_W), BF16)] * 3
        out_specs += [pl.BlockSpec((1, d, tm // d, GROUP_W), lambda b, i: (b, 0, i, 0))] * 3
    const = lambda shp: pl.BlockSpec(shp, lambda b, i: (0,) * len(shp), pipeline_mode=pl.Buffered(1))
    return pl.pallas_call(
        _proj_kernel,
        out_shape=out_shape,
        grid=(B, S // tm),
        in_specs=[pl.BlockSpec((1, tm, D), lambda b, i: (b, i, 0)),
                  const((1, D)), const(w_in.shape), const(gains.shape)],
        out_specs=out_specs,
        scratch_shapes=[pltpu.VMEM((D // 128, tm, 128), F32)],
        compiler_params=pltpu.CompilerParams(
            dimension_semantics=("arbitrary", "arbitrary"), vmem_limit_bytes=VMEM_LIMIT),
        name="proj",
    )(x, norm_w.reshape(1, D), w_in, gains)


ATT_STEP_ROWS = 2048
ATT_SUB = 2 * RADIUS


def _attn_kernel(q_ref, kp_ref, kc_ref, kn_ref, vp_ref, vc_ref, vn_ref, bias_ref, o_ref, st_ref, kbuf, vbuf):
    nb, ti = q_ref.shape[0], q_ref.shape[1]
    i = pl.program_id(1)
    first = i == 0
    last = i == pl.num_programs(1) - 1
    lane = lax.broadcasted_iota(jnp.int32, (ATT_SUB, 128), 1)
    low = lane < HEAD_DIM
    ones = jnp.ones((2 * ATT_SUB, 128), BF16)
    n_sub = ti // ATT_SUB
    for b in range(nb):
        kbuf[b, 0:RADIUS] = kp_ref[b]
        kbuf[b, RADIUS:RADIUS + ti] = kc_ref[b]
        kbuf[b, RADIUS + ti:] = kn_ref[b]
        vbuf[b, 0:RADIUS] = vp_ref[b]
        vbuf[b, RADIUS:RADIUS + ti] = vc_ref[b]
        vbuf[b, RADIUS + ti:] = vn_ref[b]
        for j in range(n_sub):
            r0 = j * ATT_SUB
            variant = 0
            if j == 0:
                variant = jnp.where(first, 1, 0)
            if j == n_sub - 1:
                variant = jnp.where(last, 2, 0)
            st = jnp.zeros((ATT_SUB, 128), F32)
            for p in range(N_SLOTS // 2):
                c0 = p * 128
                qp = q_ref[b, r0:r0 + ATT_SUB, c0:c0 + 128]
                kw = kbuf[b, r0:r0 + 2 * ATT_SUB, c0:c0 + 128]
                vw = jnp.concatenate([vbuf[b, r0:r0 + 2 * ATT_SUB, c0:c0 + 128], ones], axis=1)
                zero = jnp.zeros_like(qp)
                qs = jnp.concatenate([jnp.where(low, qp, zero), jnp.where(low, zero, qp)], axis=0)
                s = lax.dot_general(qs, kw, (((1,), (1,)), ((), ())), preferred_element_type=F32)
                s = s + bias_ref[variant, p]
                m = jnp.max(s, axis=-1, keepdims=True)
                e = jnp.exp2(s - m).astype(BF16)
                ol = jnp.dot(e, vw, preferred_element_type=F32)
                o_ref[b, r0:r0 + ATT_SUB, c0:c0 + 128] = jnp.where(
                    low, ol[:ATT_SUB, :128], ol[ATT_SUB:, :128]).astype(BF16)
                st = jnp.where(lane == 2 * p, m[:ATT_SUB], st)
                st = jnp.where(lane == 2 * p + 1, m[ATT_SUB:], st)
                st = jnp.where(lane == N_SLOTS + 2 * p, ol[:ATT_SUB, 128:], st)
                st = jnp.where(lane == N_SLOTS + 2 * p + 1, ol[ATT_SUB:, 128:], st)
            st_ref[b, r0:r0 + ATT_SUB, :] = st


def _attention(q, k, v, dilation):
    N, L, W = q.shape
    ti = min(L, ATT_STEP_ROWS)
    nb = ATT_STEP_ROWS // ti
    nblk = L // ti
    halo_per_blk = ti // RADIUS
    n_halo = L // RADIUS
    bias = jnp.asarray(_attn_bias(dilation))
    cur = pl.BlockSpec((nb, ti, W), lambda n, i: (n, i, 0))
    prev = pl.BlockSpec((nb, RADIUS, W), lambda n, i: (n, jnp.maximum(i * halo_per_blk - 1, 0), 0))
    nxt = pl.BlockSpec((nb, RADIUS, W), lambda n, i: (n, jnp.minimum((i + 1) * halo_per_blk, n_halo - 1), 0))
    return pl.pallas_call(
        _attn_kernel,
        out_shape=[jax.ShapeDtypeStruct((N, L, W), BF16), jax.ShapeDtypeStruct((N, L, 128), F32)],
        grid=(N // nb, nblk),
        in_specs=[cur, prev, cur, nxt, prev, cur, nxt,
                  pl.BlockSpec(bias.shape, lambda n, i: (0, 0, 0, 0), pipeline_mode=pl.Buffered(1))],
        out_specs=[pl.BlockSpec((nb, ti, W), lambda n, i: (n, i, 0)),
                   pl.BlockSpec((nb, ti, 128), lambda n, i: (n, i, 0))],
        scratch_shapes=[pltpu.VMEM((nb, ti + 2 * RADIUS, W), BF16), pltpu.VMEM((nb, ti + 2 * RADIUS, W), BF16)],
        compiler_params=pltpu.CompilerParams(
            dimension_semantics=("arbitrary", "arbitrary"), vmem_limit_bytes=VMEM_LIMIT),
        name=f"attn_d{dilation}",
    )(q, k, k, k, v, v, v, bias)


FFT_CW = 256
FFT_UNROLL = 4


def _fft_kernel(u_ref, g1_ref, g2_ref, g3_ref, zr_ref, zi_ref, y_ref):
    cw = u_ref.shape[-1]

    def stage1(p, carry):
        xg = u_ref[0, :, pl.ds(p, 1), :, :].reshape(NQ * NJ, cw)
        r = jnp.dot(g1_ref[p], xg, preferred_element_type=F32)
        y_ref[pl.ds(p, 1)] = r.astype(BF16).reshape(1, 2, NQ, NJ, cw)
        return carry

    lax.fori_loop(0, NP, stage1, 0, unroll=4 * FFT_UNROLL)

    def stage2(kq, carry):
        xg = y_ref[:, :, pl.ds(kq, 1), :, :].reshape(NP * 2 * NJ, cw)
        r = jnp.dot(g2_ref[...], xg, preferred_element_type=F32)
        y_ref[:, :, pl.ds(kq, 1), :, :] = r.astype(BF16).reshape(NP, 2, 1, NJ, cw)
        return carry

    lax.fori_loop(0, NQ, stage2, 0, unroll=FFT_UNROLL)

    def stage3(kp, carry):
        xg = y_ref[pl.ds(kp, 1)].reshape(2 * NQ * NJ, cw)
        r = jnp.dot(g3_ref[...], xg, preferred_element_type=F32).astype(BF16)
        half = NJ * NQ
        zr_ref[0, :, pl.ds(kp, 1), :, :] = r[:half].reshape(NJ, 1, NQ, cw)
        zi_ref[0, :, pl.ds(kp, 1), :, :] = r[half:].reshape(NJ, 1, NQ, cw)
        return carry

    lax.fori_loop(0, NP, stage3, 0, unroll=4 * FFT_UNROLL)


def _seq_dft(u):
    B, S, W = u.shape
    g1, g2, g3 = (jnp.asarray(g).astype(BF16) for g in _dft_stage_matrices())
    u5 = u.reshape(B, NQ, NP, NJ, W)
    blk = pl.BlockSpec((1, NQ, NP, NJ, FFT_CW), lambda b, c: (b, 0, 0, 0, c))
    zr, zi = pl.pallas_call(
        _fft_kernel,
        out_shape=[jax.ShapeDtypeStruct((B, NJ, NP, NQ, W), BF16)] * 2,
        grid=(B, W // FFT_CW),
        in_specs=[blk,
                  pl.BlockSpec(g1.shape, lambda b, c: (0, 0, 0), pipeline_mode=pl.Buffered(1)),
                  pl.BlockSpec(g2.shape, lambda b, c: (0, 0), pipeline_mode=pl.Buffered(1)),
                  pl.BlockSpec(g3.shape, lambda b, c: (0, 0), pipeline_mode=pl.Buffered(1))],
        out_specs=[blk, blk],
        scratch_shapes=[pltpu.VMEM((NP, 2, NQ, NJ, FFT_CW), BF16)],
        compiler_params=pltpu.CompilerParams(
            dimension_semantics=("arbitrary", "arbitrary"), vmem_limit_bytes=VMEM_LIMIT),
        name="seq_dft",
    )(u5, g1, g2, g3)
    return zr.reshape(B, S, W), zi.reshape(B, S, W)


def _fold_kernel(cd_ref, sd_ref, wf_ref, a_ref, b_ref):
    for h in range(2):
        a_ref[h] = jnp.zeros(a_ref.shape[1:], a_ref.dtype)
        b_ref[h] = jnp.zeros(b_ref.shape[1:], b_ref.dtype)
    for g in range(N_SLOTS):
        h, o = divmod(g, 4)
        w = wf_ref[g]
        a = jnp.dot(cd_ref[...], w, preferred_element_type=F32, precision=lax.Precision.HIGHEST)
        b = jnp.dot(sd_ref[...], w, preferred_element_type=F32, precision=lax.Precision.HIGHEST)
        a_ref[h, o * HEAD_DIM:(o + 1) * HEAD_DIM, o * HEAD_DIM:(o + 1) * HEAD_DIM] = a.astype(BF16)
        b_ref[h, o * HEAD_DIM:(o + 1) * HEAD_DIM, o * HEAD_DIM:(o + 1) * HEAD_DIM] = b.astype(BF16)


def _fold_fourier_weights(w_fourier):
    cd, sd = _head_dim_dft()
    shp = jax.ShapeDtypeStruct((2, 256, 256), BF16)
    return pl.pallas_call(_fold_kernel, out_shape=[shp, shp], name="fold_fourier")(
        jnp.asarray(cd), jnp.asarray(sd), w_fourier.astype(F32))


FINAL_TM = 512
PERM_ROWS = 256


def _final_kernel(x_ref, zr_ref, zi_ref, gf_ref, ga_ref, o0_ref, o1_ref, o2_ref, l0_ref, l1_ref, l2_ref,
                  a_ref, b_ref, ex_ref, p1_ref, p2_ref, wo_ref, y_ref, l1_nat, l2_nat):
    tm = FINAL_TM
    fs = []
    for h in range(2):
        sl = slice(h * 256, (h + 1) * 256)
        fs.append(jnp.dot(zr_ref[0, :, sl], a_ref[h], preferred_element_type=F32)
                  + jnp.dot(zi_ref[0, :, sl], b_ref[h], preferred_element_type=F32))
    f = jnp.concatenate(fs, axis=-1)
    gf = gf_ref[0].astype(F32)
    yf = (f * (gf * jax.nn.sigmoid(gf))).astype(BF16)

    for l_ref, l_nat, d in ((l1_ref, l1_nat, DILATIONS[1]), (l2_ref, l2_nat, DILATIONS[2])):
        for r in range(d):
            l_nat[pl.ds(r, tm // d, stride=d), :] = l_ref[0, r]
    o0 = o0_ref[0].astype(F32)

    def unpermute(o_ref, p_ref, d):
        rows = PERM_ROWS // d
        parts = []
        for k in range(tm // PERM_ROWS):
            src = jnp.concatenate([o_ref[0, r, k * rows:(k + 1) * rows, :] for r in range(d)], axis=0)
            parts.append(jnp.dot(p_ref[...], src, preferred_element_type=F32))
        return jnp.concatenate(parts, axis=0)

    o1 = unpermute(o1_ref, p1_ref, DILATIONS[1])
    o2 = unpermute(o2_ref, p2_ref, DILATIONS[2])

    sts = (l0_ref[0], l1_nat[...], l2_nat[...])
    mx = jnp.maximum(jnp.maximum(sts[0], sts[1]), sts[2])
    es = [jnp.exp2(st - mx) for st in sts]
    den = None
    for e_c, st in zip(es, sts):
        term = e_c * pltpu.roll(st, 128 - N_SLOTS, 1)
        den = term if den is None else den + term
    slot_lane = lax.broadcasted_iota(jnp.int32, den.shape, 1) < N_SLOTS
    inv = 1.0 / jnp.where(slot_lane, den, 1.0)
    ex = ex_ref[...]
    o = None
    for e_c, o_c in zip(es, (o0, o1, o2)):
        alpha = jnp.dot((e_c * inv).astype(BF16), ex, preferred_element_type=F32)
        o = alpha * o_c if o is None else o + alpha * o_c
    ga = ga_ref[0].astype(F32)
    ya = (o * (ga * jax.nn.sigmoid(ga))).astype(BF16)

    mixed = (jnp.dot(yf, wo_ref[0:GROUP_W, :], preferred_element_type=F32)
             + jnp.dot(ya, wo_ref[GROUP_W:, :], preferred_element_type=F32))
    y_ref[0] = x_ref[0] + mixed


def _final(x, zr, zi, gf, ga, o_list, lse_list, a_blk, b_blk, w_out):
    B, S, D = x.shape
    tm = FINAL_TM
    d1, d2 = DILATIONS[1], DILATIONS[2]
    nat = lambda w: pl.BlockSpec((1, tm, w), lambda b, i: (b, i, 0))
    perm = lambda d, w: pl.BlockSpec((1, d, tm // d, w), lambda b, i: (b, 0, i, 0))
    const = lambda shp: pl.BlockSpec(shp, lambda b, i: (0,) * len(shp), pipeline_mode=pl.Buffered(1))
    p1 = jnp.asarray(_unpermute_matrix(PERM_ROWS, d1)).astype(BF16)
    p2 = jnp.asarray(_unpermute_matrix(PERM_ROWS, d2)).astype(BF16)
    ex = jnp.asarray(_slot_expand_matrix()).astype(BF16)
    return pl.pallas_call(
        _final_kernel,
        out_shape=jax.ShapeDtypeStruct((B, S, D), F32),
        grid=(B, S // tm),
        in_specs=[nat(D), nat(GROUP_W), nat(GROUP_W), nat(GROUP_W), nat(GROUP_W),
                  nat(GROUP_W), perm(d1, GROUP_W), perm(d2, GROUP_W), nat(128), perm(d1, 128), perm(d2, 128),
                  const((2, 256, 256)), const((2, 256, 256)), const((128, GROUP_W)),
                  const((PERM_ROWS, PERM_ROWS)), const((PERM_ROWS, PERM_ROWS)), const((D, D))],
        out_specs=nat(D),
        scratch_shapes=[pltpu.VMEM((tm, 128), F32), pltpu.VMEM((tm, 128), F32)],
        compiler_params=pltpu.CompilerParams(
            dimension_semantics=("arbitrary", "arbitrary"), vmem_limit_bytes=VMEM_LIMIT),
        name="final_mix",
    )(x, zr, zi, gf, ga, *o_list, *lse_list, a_blk, b_blk, ex, p1, p2, w_out)


def kernel(x, norm_w, w_in, q_norm_w, k_norm_w, w_fourier, w_out):
    B, S, D = x.shape
    n_cfg = len(DILATIONS)
    xf = x.astype(F32)
    gains = jnp.concatenate([q_norm_w.astype(F32).reshape(n_cfg, GROUP_W) * (HEAD_DIM ** -0.5 * LOG2E),
                             k_norm_w.astype(F32).reshape(n_cfg, GROUP_W)], axis=0)
    proj = _project(xf, norm_w.astype(F32), w_in.astype(BF16), gains)
    u_f, g_f, g_a = proj[:3]

    outs, lses = [], []
    for c, d in enumerate(DILATIONS):
        q, k, v = (t.reshape(B * d, S // d, GROUP_W) for t in proj[3 + 3 * c:6 + 3 * c])
        o, lse = _attention(q, k, v, d)
        if d == 1:
            outs.append(o)
            lses.append(lse)
        else:
            outs.append(o.reshape(B, d, S // d, GROUP_W))
            lses.append(lse.reshape(B, d, S // d, 128))

    zr, zi = _seq_dft(u_f)
    a_blk, b_blk = _fold_fourier_weights(w_fourier)
    y = _final(xf, zr, zi, g_f, g_a, outs, lses, a_blk, b_blk, w_out.astype(BF16))
    return y.astype(x.dtype)
```

```python
import functools

import numpy as np
import jax
import jax.numpy as jnp
from jax import lax
from jax.experimental import pallas as pl
from jax.experimental.pallas import tpu as pltpu

D_MODEL = 1024
HEAD_DIM = 64
N_SLOTS = 8
GROUP_W = N_SLOTS * HEAD_DIM
DILATIONS = (1, 4, 16)
RADIUS = 64
NORM_EPS = 1e-6
MASK_VALUE = -1e30
LOG2E = float(np.log2(np.e))
SEQ = 8192

NQ, NP, NJ = 16, 32, 16

VMEM_LIMIT = 60 * 1024 * 1024

BF16 = jnp.bfloat16
F32 = jnp.float32


@functools.lru_cache(maxsize=None)
def _dft_stage_matrices():
    q = np.arange(NQ, dtype=np.float64)
    p = np.arange(NP, dtype=np.float64)
    j = np.arange(NJ, dtype=np.float64)
    eye_j = (j[:, None] == j[None, :]).astype(np.float64)

    kq = q
    base = (q[None, None, :, None] * kq[:, None, None, None] / NQ
            + j[None, :, None, None] * kq[:, None, None, None] / (NQ * NP * NJ))
    g1 = np.zeros((NP, 2, NQ, NJ, NQ, NJ))
    for pv in range(NP):
        th = 2 * np.pi * np.broadcast_to(base + pv * kq[:, None, None, None] / (NP * NJ), (NQ, NJ, NQ, NJ))
        d = eye_j[None, :, None, :]
        g1[pv, 0] = np.cos(th) * d
        g1[pv, 1] = -np.sin(th) * d
    g1 = g1.reshape(NP, 2 * NQ * NJ, NQ * NJ)

    kp = p
    ph = 2 * np.pi * np.broadcast_to(p[None, None, :, None] * kp[:, None, None, None] / NP
                                     + j[None, :, None, None] * kp[:, None, None, None] / (NP * NJ),
                                     (NP, NJ, NP, NJ))
    c = np.cos(ph) * eye_j[None, :, None, :]
    s = np.sin(ph) * eye_j[None, :, None, :]
    g2 = np.zeros((NP, 2, NJ, NP, 2, NJ))
    g2[:, 0, :, :, 0, :] = c
    g2[:, 0, :, :, 1, :] = s
    g2[:, 1, :, :, 0, :] = -s
    g2[:, 1, :, :, 1, :] = c
    g2 = g2.reshape(2 * NP * NJ, 2 * NP * NJ)

    kj = j
    eye_q = (q[:, None] == q[None, :]).astype(np.float64)
    ps = 2 * np.pi * np.broadcast_to(j[None, None, None, :] * kj[:, None, None, None] / NJ, (NJ, NQ, NQ, NJ))
    c = np.cos(ps) * eye_q[None, :, :, None]
    s = np.sin(ps) * eye_q[None, :, :, None]
    g3 = np.zeros((2, NJ, NQ, 2, NQ, NJ))
    g3[0, :, :, 0] = c
    g3[0, :, :, 1] = s
    g3[1, :, :, 0] = -s
    g3[1, :, :, 1] = c
    g3 = g3.reshape(2 * NJ * NQ, 2 * NQ * NJ)
    return tuple(np.asarray(g, dtype=np.float32) for g in (g1, g2, g3))


@functools.lru_cache(maxsize=None)
def _head_dim_dft():
    m = np.arange(HEAD_DIM, dtype=np.float64)
    ang = 2 * np.pi * m[:, None] * m[None, :] / HEAD_DIM
    scale = 1.0 / np.sqrt(SEQ * HEAD_DIM)
    return (np.asarray(np.cos(ang) * scale, dtype=np.float32), np.asarray(np.sin(ang) * scale, dtype=np.float32))


@functools.lru_cache(maxsize=None)
def _attn_bias(dilation):
    slopes = 2.0 ** (-8.0 * np.arange(1, N_SLOTS + 1, dtype=np.float64) / N_SLOTS)
    a = np.arange(2 * RADIUS)[:, None]
    c = np.arange(4 * RADIUS)[None, :]
    rel = np.abs(c - RADIUS - a)
    key_ok = (np.ones_like(c, dtype=bool), c >= RADIUS, c < 3 * RADIUS)
    tab = np.zeros((3, N_SLOTS, 2 * RADIUS, 4 * RADIUS))
    for var in range(3):
        for s in range(N_SLOTS):
            tab[var, s] = np.where((rel <= RADIUS) & key_ok[var], -slopes[s] * rel * dilation * LOG2E, MASK_VALUE)
    return np.asarray(tab.reshape(3, N_SLOTS // 2, 4 * RADIUS, 4 * RADIUS), dtype=np.float32)


@functools.lru_cache(maxsize=None)
def _slot_expand_matrix():
    r = np.arange(128)[:, None]
    c = np.arange(GROUP_W)[None, :]
    return np.asarray((r == c // HEAD_DIM), dtype=np.float32)


@functools.lru_cache(maxsize=None)
def _unpermute_matrix(tm, dilation):
    t = np.arange(tm)
    src = (t % dilation) * (tm // dilation) + t // dilation
    return np.asarray(src[:, None] == np.arange(tm)[None, :], dtype=np.float32)


PROJ_TM = 512
COL_UF, COL_GF, COL_Q, COL_K, COL_V, COL_GA = 0, 1, 2, 5, 8, 11


def _proj_kernel(x_ref, nw_ref, w_ref, gain_ref, uf_ref, gf_ref, ga_ref, *rest):
    qkv_refs, hn_ref = rest[:-1], rest[-1]
    x = x_ref[0]
    ms = jnp.mean(x * x, axis=-1, keepdims=True)
    hn32 = x * lax.rsqrt(ms + NORM_EPS) * nw_ref[...]
    n_slab = hn_ref.shape[0]
    for s in range(n_slab):
        hn_ref[s] = hn32[:, s * 128:(s + 1) * 128]
    low = lax.broadcasted_iota(jnp.int32, (PROJ_TM, 128), 1) < HEAD_DIM

    def project(hn, col, gain_row):
        acc = jnp.dot(hn, w_ref[:, col * GROUP_W:(col + 1) * GROUP_W], preferred_element_type=F32)
        if gain_row is None:
            return acc.astype(BF16)
        tiles = []
        for t in range(GROUP_W // 128):
            a = acc[:, t * 128:(t + 1) * 128]
            z = a * a
            z_lo = jnp.where(low, z, 0.0)
            ms_lo = jnp.sum(z_lo, axis=-1, keepdims=True) * (1.0 / HEAD_DIM)
            ms_hi = jnp.sum(z - z_lo, axis=-1, keepdims=True) * (1.0 / HEAD_DIM)
            scale = jnp.where(low, lax.rsqrt(ms_lo + NORM_EPS), lax.rsqrt(ms_hi + NORM_EPS))
            tiles.append(a * scale)
        return (jnp.concatenate(tiles, axis=-1) * gain_ref[gain_row:gain_row + 1, :]).astype(BF16)

    hn = hn32.astype(BF16)
    uf_ref[0] = project(hn, COL_UF, None)
    gf_ref[0] = project(hn, COL_GF, None)
    ga_ref[0] = project(hn, COL_GA, None)
    for c, d in enumerate(DILATIONS):
        q_ref, k_ref, v_ref = qkv_refs[3 * c:3 * c + 3]
        rows = PROJ_TM // d
        if d > 1:
            hn = jnp.concatenate(
                [jnp.concatenate([hn_ref[s, pl.ds(r, rows, stride=d), :] for s in range(n_slab)], axis=1)
                 for r in range(d)], axis=0).astype(BF16)
        for o_ref, col, gain_row in ((q_ref, COL_Q + c, c), (k_ref, COL_K + c, 3 + c), (v_ref, COL_V + c, None)):
            res = project(hn, col, gain_row)
            for r in range(d):
                o_ref[0, r] = res[r * rows:(r + 1) * rows]


def _project(x, norm_w, w_in, gains):
    B, S, D = x.shape
    tm = PROJ_TM
    nat = jax.ShapeDtypeStruct((B, S, GROUP_W), BF16)
    nat_spec = pl.BlockSpec((1, tm, GROUP_W), lambda b, i: (b, i, 0))
    out_shape, out_specs = [nat] * 3, [nat_spec] * 3
    for d in DILATIONS:
        out_shape += [jax.ShapeDtypeStruct((B, d, S // d, GROUP_W), BF16)] * 3
        out_specs += [pl.BlockSpec((1, d, tm // d, GROUP_W), lambda b, i: (b, 0, i, 0))] * 3
    const = lambda shp: pl.BlockSpec(shp, lambda b, i: (0,) * len(shp), pipeline_mode=pl.Buffered(1))
    return pl.pallas_call(
        _proj_kernel,
        out_shape=out_shape,
        grid=(B, S // tm),
        in_specs=[pl.BlockSpec((1, tm, D), lambda b, i: (b, i, 0)),
                  const((1, D)), const(w_in.shape), const(gains.shape)],
        out_specs=out_specs,
        scratch_shapes=[pltpu.VMEM((D // 128, tm, 128), F32)],
        compiler_params=pltpu.CompilerParams(
            dimension_semantics=("arbitrary", "arbitrary"), vmem_limit_bytes=VMEM_LIMIT),
        name="proj",
    )(x, norm_w.reshape(1, D), w_in, gains)


ATT_STEP_ROWS = 4096
ATT_MIX_STEP_ROWS = 2048
ATT_SUB = 2 * RADIUS
PERM_ROWS = 256


def _mix_groups(o_groups, st_groups, ex):
    mx = functools.reduce(jnp.maximum, st_groups)
    es = [jnp.exp2(st - mx) for st in st_groups]
    den = None
    for e_c, st in zip(es, st_groups):
        term = e_c * pltpu.roll(st, 128 - N_SLOTS, 1)
        den = term if den is None else den + term
    slot_lane = lax.broadcasted_iota(jnp.int32, den.shape, 1) < N_SLOTS
    inv = 1.0 / jnp.where(slot_lane, den, 1.0)
    o = None
    for e_c, o_c in zip(es, o_groups):
        alpha = jnp.dot((e_c * inv).astype(BF16), ex, preferred_element_type=F32)
        o = alpha * o_c if o is None else o + alpha * o_c
    return o


def _attn_kernel(q_ref, kp_ref, kc_ref, kn_ref, vp_ref, vc_ref, vn_ref, bias_ref, *rest, combine):
    if combine:
        (ga_ref, o1_ref, o2_ref, s1_ref, s2_ref, ex_ref, p1_ref, p2_ref, ya_ref, kbuf, vbuf, s1_nat, s2_nat) = rest
    else:
        o_ref, st_ref, kbuf, vbuf = rest
    nb, ti = q_ref.shape[0], q_ref.shape[1]
    i = pl.program_id(1)
    first = i == 0
    last = i == pl.num_programs(1) - 1
    lane = lax.broadcasted_iota(jnp.int32, (ATT_SUB, 128), 1)
    low = lane < HEAD_DIM
    ones = jnp.ones((2 * ATT_SUB, 128), BF16)
    n_sub = ti // ATT_SUB
    if combine:
        for s_ref, s_nat, d in ((s1_ref, s1_nat, DILATIONS[1]), (s2_ref, s2_nat, DILATIONS[2])):
            for r in range(d):
                s_nat[pl.ds(r, ti // d, stride=d), :] = s_ref[0, r]
    for b in range(nb):
        kbuf[b, 0:RADIUS] = kp_ref[b]
        kbuf[b, RADIUS:RADIUS + ti] = kc_ref[b]
        kbuf[b, RADIUS + ti:] = kn_ref[b]
        vbuf[b, 0:RADIUS] = vp_ref[b]
        vbuf[b, RADIUS:RADIUS + ti] = vc_ref[b]
        vbuf[b, RADIUS + ti:] = vn_ref[b]
        def attend(j):
            r0 = j * ATT_SUB
            variant = 0
            if j == 0:
                variant = jnp.where(first, 1, 0)
            if j == n_sub - 1:
                variant = jnp.where(last, 2, 0)
            st = jnp.zeros((ATT_SUB, 128), F32)
            o_tiles = []
            for p in range(N_SLOTS // 2):
                c0 = p * 128
                qp = q_ref[b, r0:r0 + ATT_SUB, c0:c0 + 128]
                kw = kbuf[b, r0:r0 + 2 * ATT_SUB, c0:c0 + 128]
                vw = jnp.concatenate([vbuf[b, r0:r0 + 2 * ATT_SUB, c0:c0 + 128], ones], axis=1)
                zero = jnp.zeros_like(qp)
                qs = jnp.concatenate([jnp.where(low, qp, zero), jnp.where(low, zero, qp)], axis=0)
                s = lax.dot_general(qs, kw, (((1,), (1,)), ((), ())), preferred_element_type=F32)
                s = s + bias_ref[variant, p]
                m = jnp.max(s, axis=-1, keepdims=True)
                e = jnp.exp2(s - m).astype(BF16)
                ol = jnp.dot(e, vw, preferred_element_type=F32)
                o_tiles.append(jnp.where(low, ol[:ATT_SUB, :128], ol[ATT_SUB:, :128]))
                st = jnp.where(lane == 2 * p, m[:ATT_SUB], st)
                st = jnp.where(lane == 2 * p + 1, m[ATT_SUB:], st)
                st = jnp.where(lane == N_SLOTS + 2 * p, ol[:ATT_SUB, 128:], st)
                st = jnp.where(lane == N_SLOTS + 2 * p + 1, ol[ATT_SUB:, 128:], st)
            return jnp.concatenate(o_tiles, axis=-1), st

        if not combine:
            for j in range(n_sub):
                o, st = attend(j)
                o_ref[b, j * ATT_SUB:(j + 1) * ATT_SUB, :] = o.astype(BF16)
                st_ref[b, j * ATT_SUB:(j + 1) * ATT_SUB, :] = st
            continue
        per_sub = PERM_ROWS // ATT_SUB
        n_mix = ti // PERM_ROWS
        pending = None
        for jj in range(n_mix + 1):
            done, pending = pending, (
                [attend(per_sub * jj + t) for t in range(per_sub)] if jj < n_mix else None)
            if done is None:
                continue
            parts = done
            rows = slice((jj - 1) * PERM_ROWS, jj * PERM_ROWS)
            o_groups = [jnp.concatenate([o for o, _ in parts], axis=0)]
            for o_c_ref, p_ref, d in ((o1_ref, p1_ref, DILATIONS[1]), (o2_ref, p2_ref, DILATIONS[2])):
                n = PERM_ROWS // d
                src = jnp.concatenate([o_c_ref[0, r, (jj - 1) * n:jj * n, :] for r in range(d)], axis=0)
                o_groups.append(jnp.dot(p_ref[...], src, preferred_element_type=F32))
            st0 = jnp.concatenate([st for _, st in parts], axis=0)
            o = _mix_groups(o_groups, (st0, s1_nat[rows, :], s2_nat[rows, :]), ex_ref[...])
            ga = ga_ref[0, rows, :].astype(F32)
            ya_ref[0, rows, :] = (o * (ga * jax.nn.sigmoid(ga))).astype(BF16)


def _attention(q, k, v, dilation, mix=None):
    N, L, W = q.shape
    combine = mix is not None
    step_rows = ATT_MIX_STEP_ROWS if combine else ATT_STEP_ROWS
    ti = min(L, step_rows)
    nb = step_rows // ti
    nblk = L // ti
    halo_per_blk = ti // RADIUS
    n_halo = L // RADIUS
    bias = jnp.asarray(_attn_bias(dilation))
    const = lambda shp: pl.BlockSpec(shp, lambda n, i: (0,) * len(shp), pipeline_mode=pl.Buffered(1))
    cur = pl.BlockSpec((nb, ti, W), lambda n, i: (n, i, 0))
    prev = pl.BlockSpec((nb, RADIUS, W), lambda n, i: (n, jnp.maximum(i * halo_per_blk - 1, 0), 0))
    nxt = pl.BlockSpec((nb, RADIUS, W), lambda n, i: (n, jnp.minimum((i + 1) * halo_per_blk, n_halo - 1), 0))
    in_specs = [cur, prev, cur, nxt, prev, cur, nxt, const(bias.shape)]
    operands = [q, k, k, k, v, v, v, bias]
    scratch = [pltpu.VMEM((nb, ti + 2 * RADIUS, W), BF16), pltpu.VMEM((nb, ti + 2 * RADIUS, W), BF16)]
    if combine:
        assert dilation == 1 and nb == 1
        d1, d2 = DILATIONS[1], DILATIONS[2]
        perm = lambda d, w: pl.BlockSpec((1, d, ti // d, w), lambda n, i: (n, 0, i, 0))
        in_specs += [cur, perm(d1, W), perm(d2, W), perm(d1, 128), perm(d2, 128),
                     const((128, W)), const((PERM_ROWS, PERM_ROWS)), const((PERM_ROWS, PERM_ROWS))]
        operands += [*mix, jnp.asarray(_slot_expand_matrix()).astype(BF16),
                     jnp.asarray(_unpermute_matrix(PERM_ROWS, d1)).astype(BF16),
                     jnp.asarray(_unpermute_matrix(PERM_ROWS, d2)).astype(BF16)]
        out_shape = jax.ShapeDtypeStruct((N, L, W), BF16)
        out_specs = cur
        scratch += [pltpu.VMEM((ti, 128), F32), pltpu.VMEM((ti, 128), F32)]
    else:
        out_shape = [jax.ShapeDtypeStruct((N, L, W), BF16), jax.ShapeDtypeStruct((N, L, 128), F32)]
        out_specs = [cur, pl.BlockSpec((nb, ti, 128), lambda n, i: (n, i, 0))]
    return pl.pallas_call(
        functools.partial(_attn_kernel, combine=combine),
        out_shape=out_shape,
        grid=(N // nb, nblk),
        in_specs=in_specs,
        out_specs=out_specs,
        scratch_shapes=scratch,
        compiler_params=pltpu.CompilerParams(
            dimension_semantics=("arbitrary", "arbitrary"), vmem_limit_bytes=VMEM_LIMIT),
        name=f"attn_d{dilation}",
    )(*operands)


FFT_CW = 256
FFT_UNROLL = 4


def _fft_kernel(u_ref, g1_ref, g2_ref, g3_ref, zr_ref, zi_ref, y_ref):
    cw = u_ref.shape[-1]

    def stage1(p, carry):
        xg = u_ref[0, :, pl.ds(p, 1), :, :].reshape(NQ * NJ, cw)
        r = jnp.dot(g1_ref[p], xg, preferred_element_type=F32)
        y_ref[pl.ds(p, 1)] = r.astype(BF16).reshape(1, 2, NQ, NJ, cw)
        return carry

    lax.fori_loop(0, NP, stage1, 0, unroll=4 * FFT_UNROLL)

    def stage2(kq, carry):
        xg = y_ref[:, :, pl.ds(kq, 1), :, :].reshape(NP * 2 * NJ, cw)
        r = jnp.dot(g2_ref[...], xg, preferred_element_type=F32)
        y_ref[:, :, pl.ds(kq, 1), :, :] = r.astype(BF16).reshape(NP, 2, 1, NJ, cw)
        return carry

    lax.fori_loop(0, NQ, stage2, 0, unroll=FFT_UNROLL)

    def stage3(kp, carry):
        xg = y_ref[pl.ds(kp, 1)].reshape(2 * NQ * NJ, cw)
        r = jnp.dot(g3_ref[...], xg, preferred_element_type=F32).astype(BF16)
        half = NJ * NQ
        zr_ref[0, :, pl.ds(kp, 1), :, :] = r[:half].reshape(NJ, 1, NQ, cw)
        zi_ref[0, :, pl.ds(kp, 1), :, :] = r[half:].reshape(NJ, 1, NQ, cw)
        return carry

    lax.fori_loop(0, NP, stage3, 0, unroll=4 * FFT_UNROLL)


def _seq_dft(u):
    B, S, W = u.shape
    g1, g2, g3 = (jnp.asarray(g).astype(BF16) for g in _dft_stage_matrices())
    u5 = u.reshape(B, NQ, NP, NJ, W)
    blk = pl.BlockSpec((1, NQ, NP, NJ, FFT_CW), lambda b, c: (b, 0, 0, 0, c))
    zr, zi = pl.pallas_call(
        _fft_kernel,
        out_shape=[jax.ShapeDtypeStruct((B, NJ, NP, NQ, W), BF16)] * 2,
        grid=(B, W // FFT_CW),
        in_specs=[blk,
                  pl.BlockSpec(g1.shape, lambda b, c: (0, 0, 0), pipeline_mode=pl.Buffered(1)),
                  pl.BlockSpec(g2.shape, lambda b, c: (0, 0), pipeline_mode=pl.Buffered(1)),
                  pl.BlockSpec(g3.shape, lambda b, c: (0, 0), pipeline_mode=pl.Buffered(1))],
        out_specs=[blk, blk],
        scratch_shapes=[pltpu.VMEM((NP, 2, NQ, NJ, FFT_CW), BF16)],
        compiler_params=pltpu.CompilerParams(
            dimension_semantics=("arbitrary", "arbitrary"), vmem_limit_bytes=VMEM_LIMIT),
        name="seq_dft",
    )(u5, g1, g2, g3)
    return zr.reshape(B, S, W), zi.reshape(B, S, W)


def _fold_kernel(cd_ref, sd_ref, wf_ref, a_ref, b_ref):
    for h in range(2):
        a_ref[h] = jnp.zeros(a_ref.shape[1:], a_ref.dtype)
        b_ref[h] = jnp.zeros(b_ref.shape[1:], b_ref.dtype)
    for g in range(N_SLOTS):
        h, o = divmod(g, 4)
        w = wf_ref[g]
        a = jnp.dot(cd_ref[...], w, preferred_element_type=F32, precision=lax.Precision.HIGHEST)
        b = jnp.dot(sd_ref[...], w, preferred_element_type=F32, precision=lax.Precision.HIGHEST)
        a_ref[h, o * HEAD_DIM:(o + 1) * HEAD_DIM, o * HEAD_DIM:(o + 1) * HEAD_DIM] = a.astype(BF16)
        b_ref[h, o * HEAD_DIM:(o + 1) * HEAD_DIM, o * HEAD_DIM:(o + 1) * HEAD_DIM] = b.astype(BF16)


def _fold_fourier_weights(w_fourier):
    cd, sd = _head_dim_dft()
    shp = jax.ShapeDtypeStruct((2, 256, 256), BF16)
    return pl.pallas_call(_fold_kernel, out_shape=[shp, shp], name="fold_fourier")(
        jnp.asarray(cd), jnp.asarray(sd), w_fourier.astype(F32))


FINAL_TM = 512


def _final_kernel(x_ref, zr_ref, zi_ref, gf_ref, ya_ref, a_ref, b_ref, wo_ref, y_ref):
    fs = []
    for h in range(2):
        sl = slice(h * 256, (h + 1) * 256)
        fs.append(jnp.dot(zr_ref[0, :, sl], a_ref[h], preferred_element_type=F32)
                  + jnp.dot(zi_ref[0, :, sl], b_ref[h], preferred_element_type=F32))
    f = jnp.concatenate(fs, axis=-1)
    gf = gf_ref[0].astype(F32)
    yf = (f * (gf * jax.nn.sigmoid(gf))).astype(BF16)

    mixed = (jnp.dot(yf, wo_ref[0:GROUP_W, :], preferred_element_type=F32)
             + jnp.dot(ya_ref[0], wo_ref[GROUP_W:, :], preferred_element_type=F32))
    y_ref[0] = x_ref[0] + mixed


def _final(x, zr, zi, gf, ya, a_blk, b_blk, w_out):
    B, S, D = x.shape
    tm = FINAL_TM
    nat = lambda w: pl.BlockSpec((1, tm, w), lambda b, i: (b, i, 0))
    const = lambda shp: pl.BlockSpec(shp, lambda b, i: (0,) * len(shp), pipeline_mode=pl.Buffered(1))
    return pl.pallas_call(
        _final_kernel,
        out_shape=jax.ShapeDtypeStruct((B, S, D), F32),
        grid=(B, S // tm),
        in_specs=[nat(D), nat(GROUP_W), nat(GROUP_W), nat(GROUP_W), nat(GROUP_W),
                  const((2, 256, 256)), const((2, 256, 256)), const((D, D))],
        out_specs=nat(D),
        compiler_params=pltpu.CompilerParams(
            dimension_semantics=("arbitrary", "arbitrary"), vmem_limit_bytes=VMEM_LIMIT),
        name="out_proj",
    )(x, zr, zi, gf, ya, a_blk, b_blk, w_out)


def kernel(x, norm_w, w_in, q_norm_w, k_norm_w, w_fourier, w_out):
    B, S, D = x.shape
    n_cfg = len(DILATIONS)
    xf = x.astype(F32)
    gains = jnp.concatenate([q_norm_w.astype(F32).reshape(n_cfg, GROUP_W) * (HEAD_DIM ** -0.5 * LOG2E),
                             k_norm_w.astype(F32).reshape(n_cfg, GROUP_W)], axis=0)
    proj = _project(xf, norm_w.astype(F32), w_in.astype(BF16), gains)
    u_f, g_f, g_a = proj[:3]

    dilated = []
    for c, d in list(enumerate(DILATIONS))[1:]:
        q, k, v = (t.reshape(B * d, S // d, GROUP_W) for t in proj[3 + 3 * c:6 + 3 * c])
        o, st = _attention(q, k, v, d)
        dilated.append((o.reshape(B, d, S // d, GROUP_W), st.reshape(B, d, S // d, 128)))
    (o1, st1), (o2, st2) = dilated
    q, k, v = (t.reshape(B, S, GROUP_W) for t in proj[3:6])
    y_a = _attention(q, k, v, DILATIONS[0], mix=(g_a, o1, o2, st1, st2))

    zr, zi = _seq_dft(u_f)
    a_blk, b_blk = _fold_fourier_weights(w_fourier)
    y = _final(xf, zr, zi, g_f, y_a, a_blk, b_blk, w_out.astype(BF16))
    return y.astype(x.dtype)
```

```python
import functools

import numpy as np
import jax
import jax.numpy as jnp
from jax import lax
from jax.experimental import pallas as pl
from jax.experimental.pallas import tpu as pltpu

D_MODEL = 1024
HEAD_DIM = 64
N_SLOTS = 8
GROUP_W = N_SLOTS * HEAD_DIM
DILATIONS = (1, 4, 16)
RADIUS = 64
NORM_EPS = 1e-6
MASK_VALUE = -1e30
LOG2E = float(np.log2(np.e))
SEQ = 8192

NR, NJ = 8, 16

VMEM_LIMIT = 60 * 1024 * 1024

BF16 = jnp.bfloat16
F32 = jnp.float32


def _complex_block(phase, delta):
    c = np.cos(2 * np.pi * phase) * delta
    s = np.sin(2 * np.pi * phase) * delta
    return c, s


@functools.lru_cache(maxsize=None)
def _dft_stage_matrices():
    r = np.arange(NR, dtype=np.float64)
    j = np.arange(NJ, dtype=np.float64)
    eye_j = (j[:, None] == j[None, :]).astype(np.float64)
    n_total = NJ * NR ** 3

    ka = r[:, None, None, None]
    base = r[None, None, :, None] * ka / NR + j[None, :, None, None] * ka / n_total
    ga = np.zeros((NR, NR, 2, NR, NJ, NR, NJ))
    for bv in range(NR):
        for cv in range(NR):
            ph = np.broadcast_to(base + (bv / NR ** 2 + cv / NR ** 3) * ka, (NR, NJ, NR, NJ))
            c, s = _complex_block(ph, eye_j[None, :, None, :])
            ga[bv, cv, 0] = c
            ga[bv, cv, 1] = -s
    ga = ga.reshape(NR * NR, 2 * NR * NJ, NR * NJ)

    def radix_stage(phase):
        c, s = _complex_block(np.broadcast_to(phase, (NR, NJ, NR, NJ)), eye_j[None, :, None, :])
        g = np.zeros((NR, 2, NJ, NR, 2, NJ))
        g[:, 0, :, :, 0, :] = c
        g[:, 0, :, :, 1, :] = s
        g[:, 1, :, :, 0, :] = -s
        g[:, 1, :, :, 1, :] = c
        return g.reshape(2 * NR * NJ, 2 * NR * NJ)

    kk = r[:, None, None, None]
    dig = r[None, None, :, None]
    jj = j[None, :, None, None]
    gb = np.stack([radix_stage(dig * kk / NR + cv * kk / NR ** 2 + jj * kk / (NJ * NR ** 2)) for cv in range(NR)])
    gc = radix_stage(dig * kk / NR + jj * kk / (NJ * NR))

    ph = j[None, :] * j[:, None] / NJ
    c, s = np.cos(2 * np.pi * ph), np.sin(2 * np.pi * ph)
    gj = np.zeros((2, NJ, 2, NR, 2, 2, NR, NJ))
    for kb in range(2):
        for kav in range(NR):
            gj[0, :, kb, kav, kb, 0, kav, :] = c
            gj[0, :, kb, kav, kb, 1, kav, :] = s
            gj[1, :, kb, kav, kb, 0, kav, :] = -s
            gj[1, :, kb, kav, kb, 1, kav, :] = c
    gj = gj.reshape(4 * NR * NJ, 4 * NR * NJ)
    return tuple(np.asarray(g, dtype=np.float32) for g in (ga, gb, gc, gj))


@functools.lru_cache(maxsize=None)
def _head_dim_dft():
    m = np.arange(HEAD_DIM, dtype=np.float64)
    ang = 2 * np.pi * m[:, None] * m[None, :] / HEAD_DIM
    scale = 1.0 / np.sqrt(SEQ * HEAD_DIM)
    return (np.asarray(np.cos(ang) * scale, dtype=np.float32), np.asarray(np.sin(ang) * scale, dtype=np.float32))


@functools.lru_cache(maxsize=None)
def _attn_bias(dilation):
    slopes = 2.0 ** (-8.0 * np.arange(1, N_SLOTS + 1, dtype=np.float64) / N_SLOTS)
    a = np.arange(2 * RADIUS)[:, None]
    c = np.arange(4 * RADIUS)[None, :]
    rel = np.abs(c - RADIUS - a)
    key_ok = (np.ones_like(c, dtype=bool), c >= RADIUS, c < 3 * RADIUS)
    tab = np.zeros((3, N_SLOTS, 2 * RADIUS, 4 * RADIUS))
    for var in range(3):
        for s in range(N_SLOTS):
            tab[var, s] = np.where((rel <= RADIUS) & key_ok[var], -slopes[s] * rel * dilation * LOG2E, MASK_VALUE)
    return np.asarray(tab.reshape(3, N_SLOTS // 2, 4 * RADIUS, 4 * RADIUS), dtype=np.float32)


@functools.lru_cache(maxsize=None)
def _slot_expand_matrix():
    r = np.arange(128)[:, None]
    c = np.arange(GROUP_W)[None, :]
    return np.asarray((r == c // HEAD_DIM), dtype=np.float32)


@functools.lru_cache(maxsize=None)
def _unpermute_matrix(tm, dilation):
    t = np.arange(tm)
    src = (t % dilation) * (tm // dilation) + t // dilation
    return np.asarray(src[:, None] == np.arange(tm)[None, :], dtype=np.float32)


PROJ_TM = 512
COL_UF, COL_GF, COL_Q, COL_K, COL_V, COL_GA = 0, 1, 2, 5, 8, 11


def _proj_kernel(x_ref, nw_ref, w_ref, gain_ref, uf_ref, gf_ref, ga_ref, *rest):
    qkv_refs, hn_ref = rest[:-1], rest[-1]
    x = x_ref[0]
    ms = jnp.mean(x * x, axis=-1, keepdims=True)
    hn32 = x * lax.rsqrt(ms + NORM_EPS) * nw_ref[...]
    n_slab = hn_ref.shape[0]
    for s in range(n_slab):
        hn_ref[s] = hn32[:, s * 128:(s + 1) * 128]
    low = lax.broadcasted_iota(jnp.int32, (PROJ_TM, 128), 1) < HEAD_DIM

    def project(hn, col, gain_row):
        acc = jnp.dot(hn, w_ref[:, col * GROUP_W:(col + 1) * GROUP_W], preferred_element_type=F32)
        if gain_row is None:
            return acc.astype(BF16)
        tiles = []
        for t in range(GROUP_W // 128):
            a = acc[:, t * 128:(t + 1) * 128]
            z = a * a
            z_lo = jnp.where(low, z, 0.0)
            ms_lo = jnp.sum(z_lo, axis=-1, keepdims=True) * (1.0 / HEAD_DIM)
            ms_hi = jnp.sum(z - z_lo, axis=-1, keepdims=True) * (1.0 / HEAD_DIM)
            scale = jnp.where(low, lax.rsqrt(ms_lo + NORM_EPS), lax.rsqrt(ms_hi + NORM_EPS))
            tiles.append(a * scale)
        return (jnp.concatenate(tiles, axis=-1) * gain_ref[gain_row:gain_row + 1, :]).astype(BF16)

    hn = hn32.astype(BF16)
    uf_ref[0] = project(hn, COL_UF, None)
    gf_ref[0] = project(hn, COL_GF, None)
    ga_ref[0] = project(hn, COL_GA, None)
    for c, d in enumerate(DILATIONS):
        q_ref, k_ref, v_ref = qkv_refs[3 * c:3 * c + 3]
        rows = PROJ_TM // d
        if d > 1:
            hn = jnp.concatenate(
                [jnp.concatenate([hn_ref[s, pl.ds(r, rows, stride=d), :] for s in range(n_slab)], axis=1)
                 for r in range(d)], axis=0).astype(BF16)
        for o_ref, col, gain_row in ((q_ref, COL_Q + c, c), (k_ref, COL_K + c, 3 + c), (v_ref, COL_V + c, None)):
            res = project(hn, col, gain_row)
            for r in range(d):
                o_ref[0, r] = res[r * rows:(r + 1) * rows]


def _project(x, norm_w, w_in, gains):
    B, S, D = x.shape
    tm = PROJ_TM
    nat = jax.ShapeDtypeStruct((B, S, GROUP_W), BF16)
    nat_spec = pl.BlockSpec((1, tm, GROUP_W), lambda b, i: (b, i, 0))
    out_shape, out_specs = [nat] * 3, [nat_spec] * 3
    for d in DILATIONS:
        out_shape += [jax.ShapeDtypeStruct((B, d, S // d, GROUP_W), BF16)] * 3
        out_specs += [pl.BlockSpec((1, d, tm // d, GROUP_W), lambda b, i: (b, 0, i, 0))] * 3
    const = lambda shp: pl.BlockSpec(shp, lambda b, i: (0,) * len(shp), pipeline_mode=pl.Buffered(1))
    return pl.pallas_call(
        _proj_kernel,
        out_shape=out_shape,
        grid=(B, S // tm),
        in_specs=[pl.BlockSpec((1, tm, D), lambda b, i: (b, i, 0)),
                  const((1, D)), const(w_in.shape), const(gains.shape)],
        out_specs=out_specs,
        scratch_shapes=[pltpu.VMEM((D // 128, tm, 128), F32)],
        compiler_params=pltpu.CompilerParams(
            dimension_semantics=("arbitrary", "arbitrary"), vmem_limit_bytes=VMEM_LIMIT),
        name="proj",
    )(x, norm_w.reshape(1, D), w_in, gains)


ATT_STEP_ROWS = 4096
ATT_SUB = 2 * RADIUS


def _attn_kernel(q_ref, kp_ref, kc_ref, kn_ref, vp_ref, vc_ref, vn_ref, bias_ref, o_ref, st_ref, kbuf, vbuf):
    nb, ti = q_ref.shape[0], q_ref.shape[1]
    i = pl.program_id(1)
    first = i == 0
    last = i == pl.num_programs(1) - 1
    lane = lax.broadcasted_iota(jnp.int32, (ATT_SUB, 128), 1)
    low = lane < HEAD_DIM
    ones = jnp.ones((2 * ATT_SUB, 128), BF16)
    n_sub = ti // ATT_SUB
    for b in range(nb):
        kbuf[b, 0:RADIUS] = kp_ref[b]
        kbuf[b, RADIUS:RADIUS + ti] = kc_ref[b]
        kbuf[b, RADIUS + ti:] = kn_ref[b]
        vbuf[b, 0:RADIUS] = vp_ref[b]
        vbuf[b, RADIUS:RADIUS + ti] = vc_ref[b]
        vbuf[b, RADIUS + ti:] = vn_ref[b]
        for j in range(n_sub):
            r0 = j * ATT_SUB
            variant = 0
            if j == 0:
                variant = jnp.where(first, 1, 0)
            if j == n_sub - 1:
                variant = jnp.where(last, 2, 0)
            st = jnp.zeros((ATT_SUB, 128), F32)
            for p in range(N_SLOTS // 2):
                c0 = p * 128
                qp = q_ref[b, r0:r0 + ATT_SUB, c0:c0 + 128]
                kw = kbuf[b, r0:r0 + 2 * ATT_SUB, c0:c0 + 128]
                vw = jnp.concatenate([vbuf[b, r0:r0 + 2 * ATT_SUB, c0:c0 + 128], ones], axis=1)
                zero = jnp.zeros_like(qp)
                qs = jnp.concatenate([jnp.where(low, qp, zero), jnp.where(low, zero, qp)], axis=0)
                s = lax.dot_general(qs, kw, (((1,), (1,)), ((), ())), preferred_element_type=F32)
                s = s + bias_ref[variant, p]
                m = jnp.max(s, axis=-1, keepdims=True)
                e = jnp.exp2(s - m).astype(BF16)
                ol = jnp.dot(e, vw, preferred_element_type=F32)
                o_ref[b, r0:r0 + ATT_SUB, c0:c0 + 128] = jnp.where(
                    low, ol[:ATT_SUB, :128], ol[ATT_SUB:, :128]).astype(BF16)
                st = jnp.where(lane == 2 * p, m[:ATT_SUB], st)
                st = jnp.where(lane == 2 * p + 1, m[ATT_SUB:], st)
                st = jnp.where(lane == N_SLOTS + 2 * p, ol[:ATT_SUB, 128:], st)
                st = jnp.where(lane == N_SLOTS + 2 * p + 1, ol[ATT_SUB:, 128:], st)
            st_ref[b, r0:r0 + ATT_SUB, :] = st


def _attention(q, k, v, dilation):
    N, L, W = q.shape
    ti = min(L, ATT_STEP_ROWS)
    nb = ATT_STEP_ROWS // ti
    nblk = L // ti
    halo_per_blk = ti // RADIUS
    n_halo = L // RADIUS
    bias = jnp.asarray(_attn_bias(dilation))
    cur = pl.BlockSpec((nb, ti, W), lambda n, i: (n, i, 0))
    prev = pl.BlockSpec((nb, RADIUS, W), lambda n, i: (n, jnp.maximum(i * halo_per_blk - 1, 0), 0))
    nxt = pl.BlockSpec((nb, RADIUS, W), lambda n, i: (n, jnp.minimum((i + 1) * halo_per_blk, n_halo - 1), 0))
    return pl.pallas_call(
        _attn_kernel,
        out_shape=[jax.ShapeDtypeStruct((N, L, W), BF16), jax.ShapeDtypeStruct((N, L, 128), F32)],
        grid=(N // nb, nblk),
        in_specs=[cur, prev, cur, nxt, prev, cur, nxt,
                  pl.BlockSpec(bias.shape, lambda n, i: (0, 0, 0, 0), pipeline_mode=pl.Buffered(1))],
        out_specs=[pl.BlockSpec((nb, ti, W), lambda n, i: (n, i, 0)),
                   pl.BlockSpec((nb, ti, 128), lambda n, i: (n, i, 0))],
        scratch_shapes=[pltpu.VMEM((nb, ti + 2 * RADIUS, W), BF16), pltpu.VMEM((nb, ti + 2 * RADIUS, W), BF16)],
        compiler_params=pltpu.CompilerParams(
            dimension_semantics=("arbitrary", "arbitrary"), vmem_limit_bytes=VMEM_LIMIT),
        name=f"attn_d{dilation}",
    )(q, k, k, k, v, v, v, bias)


FFT_CW = 256
FFT_UNROLL = 32


def _fft_kernel(u_ref, ga_ref, gb_ref, gc_ref, gj_ref, zr_ref, zi_ref, y_ref, y2_ref):
    cw = u_ref.shape[-1]
    grp = NR * 2 * NJ

    def stage_a(g, carry):
        b, c = g // NR, g % NR
        xg = u_ref[0, :, pl.ds(b, 1), pl.ds(c, 1), :, :].reshape(NR * NJ, cw)
        r = jnp.dot(ga_ref[g], xg, preferred_element_type=F32)
        y_ref[pl.ds(b, 1), pl.ds(c, 1)] = r.astype(BF16).reshape(1, 1, 2, NR, NJ, cw)
        return carry

    lax.fori_loop(0, NR * NR, stage_a, 0, unroll=FFT_UNROLL)

    def stage_b(g, carry):
        c, ka = g // NR, g % NR
        xg = y_ref[:, pl.ds(c, 1), :, pl.ds(ka, 1), :, :].reshape(grp, cw)
        r = jnp.dot(gb_ref[c], xg, preferred_element_type=F32)
        y2_ref[:, pl.ds(c, 1), :, pl.ds(ka, 1), :, :] = r.astype(BF16).reshape(NR, 1, 2, 1, NJ, cw)
        return carry

    lax.fori_loop(0, NR * NR, stage_b, 0, unroll=FFT_UNROLL)

    def stage_c(g, carry):
        kb, ka = g // NR, g % NR
        xg = y2_ref[pl.ds(kb, 1), :, :, pl.ds(ka, 1), :, :].reshape(grp, cw)
        r = jnp.dot(gc_ref[...], xg, preferred_element_type=F32)
        y_ref[pl.ds(kb, 1), :, :, pl.ds(ka, 1), :, :] = r.astype(BF16).reshape(1, NR, 2, 1, NJ, cw)
        return carry

    lax.fori_loop(0, NR * NR, stage_c, 0, unroll=FFT_UNROLL)

    def stage_j(g, carry):
        m, kc = g // NR, g % NR
        xg = y_ref[pl.ds(2 * m, 2), pl.ds(kc, 1)].reshape(2 * grp, cw)
        r = jnp.dot(gj_ref[...], xg, preferred_element_type=F32).astype(BF16)
        zr_ref[0, :, pl.ds(kc, 1), pl.ds(m, 1), :, :] = r[:grp].reshape(NJ, 1, 1, 2 * NR, cw)
        zi_ref[0, :, pl.ds(kc, 1), pl.ds(m, 1), :, :] = r[grp:].reshape(NJ, 1, 1, 2 * NR, cw)
        return carry

    lax.fori_loop(0, NR * NR // 2, stage_j, 0, unroll=FFT_UNROLL // 2)


def _seq_dft(u):
    B, S, W = u.shape
    mats = [jnp.asarray(g).astype(BF16) for g in _dft_stage_matrices()]
    u6 = u.reshape(B, NR, NR, NR, NJ, W)
    in_blk = pl.BlockSpec((1, NR, NR, NR, NJ, FFT_CW), lambda b, c: (b, 0, 0, 0, 0, c))
    out_blk = pl.BlockSpec((1, NJ, NR, NR // 2, 2 * NR, FFT_CW), lambda b, c: (b, 0, 0, 0, 0, c))
    const = lambda shp: pl.BlockSpec(shp, lambda b, c: (0,) * len(shp), pipeline_mode=pl.Buffered(1))
    zr, zi = pl.pallas_call(
        _fft_kernel,
        out_shape=[jax.ShapeDtypeStruct((B, NJ, NR, NR // 2, 2 * NR, W), BF16)] * 2,
        grid=(B, W // FFT_CW),
        in_specs=[in_blk] + [const(m.shape) for m in mats],
        out_specs=[out_blk, out_blk],
        scratch_shapes=[pltpu.VMEM((NR, NR, 2, NR, NJ, FFT_CW), BF16)] * 2,
        compiler_params=pltpu.CompilerParams(
            dimension_semantics=("arbitrary", "arbitrary"), vmem_limit_bytes=VMEM_LIMIT),
        name="seq_dft",
    )(u6, *mats)
    return zr.reshape(B, S, W), zi.reshape(B, S, W)


def _fold_kernel(cd_ref, sd_ref, wf_ref, a_ref, b_ref):
    for h in range(2):
        a_ref[h] = jnp.zeros(a_ref.shape[1:], a_ref.dtype)
        b_ref[h] = jnp.zeros(b_ref.shape[1:], b_ref.dtype)
    for g in range(N_SLOTS):
        h, o = divmod(g, 4)
        w = wf_ref[g]
        a = jnp.dot(cd_ref[...], w, preferred_element_type=F32, precision=lax.Precision.HIGHEST)
        b = jnp.dot(sd_ref[...], w, preferred_element_type=F32, precision=lax.Precision.HIGHEST)
        a_ref[h, o * HEAD_DIM:(o + 1) * HEAD_DIM, o * HEAD_DIM:(o + 1) * HEAD_DIM] = a.astype(BF16)
        b_ref[h, o * HEAD_DIM:(o + 1) * HEAD_DIM, o * HEAD_DIM:(o + 1) * HEAD_DIM] = b.astype(BF16)


def _fold_fourier_weights(w_fourier):
    cd, sd = _head_dim_dft()
    shp = jax.ShapeDtypeStruct((2, 256, 256), BF16)
    return pl.pallas_call(_fold_kernel, out_shape=[shp, shp], name="fold_fourier")(
        jnp.asarray(cd), jnp.asarray(sd), w_fourier.astype(F32))


FINAL_TM = 1024
PERM_ROWS = 256


def _final_kernel(x_ref, zr_ref, zi_ref, gf_ref, ga_ref, o0_ref, o1_ref, o2_ref, l0_ref, l1_ref, l2_ref,
                  a_ref, b_ref, ex_ref, p1_ref, p2_ref, wo_ref, y_ref, l1_nat, l2_nat):
    tm = FINAL_TM
    fs = []
    for h in range(2):
        sl = slice(h * 256, (h + 1) * 256)
        fs.append(jnp.dot(zr_ref[0, :, sl], a_ref[h], preferred_element_type=F32)
                  + jnp.dot(zi_ref[0, :, sl], b_ref[h], preferred_element_type=F32))
    f = jnp.concatenate(fs, axis=-1)
    gf = gf_ref[0].astype(F32)
    yf = (f * (gf * jax.nn.sigmoid(gf))).astype(BF16)

    for l_ref, l_nat, d in ((l1_ref, l1_nat, DILATIONS[1]), (l2_ref, l2_nat, DILATIONS[2])):
        for r in range(d):
            l_nat[pl.ds(r, tm // d, stride=d), :] = l_ref[0, r]
    o0 = o0_ref[0].astype(F32)

    def unpermute(o_ref, p_ref, d):
        rows = PERM_ROWS // d
        parts = []
        for k in range(tm // PERM_ROWS):
            src = jnp.concatenate([o_ref[0, r, k * rows:(k + 1) * rows, :] for r in range(d)], axis=0)
            parts.append(jnp.dot(p_ref[...], src, preferred_element_type=F32))
        return jnp.concatenate(parts, axis=0)

    o1 = unpermute(o1_ref, p1_ref, DILATIONS[1])
    o2 = unpermute(o2_ref, p2_ref, DILATIONS[2])

    sts = (l0_ref[0], l1_nat[...], l2_nat[...])
    mx = jnp.maximum(jnp.maximum(sts[0], sts[1]), sts[2])
    es = [jnp.exp2(st - mx) for st in sts]
    den = None
    for e_c, st in zip(es, sts):
        term = e_c * pltpu.roll(st, 128 - N_SLOTS, 1)
        den = term if den is None else den + term
    slot_lane = lax.broadcasted_iota(jnp.int32, den.shape, 1) < N_SLOTS
    inv = 1.0 / jnp.where(slot_lane, den, 1.0)
    ex = ex_ref[...]
    o = None
    for e_c, o_c in zip(es, (o0, o1, o2)):
        alpha = jnp.dot((e_c * inv).astype(BF16), ex, preferred_element_type=F32)
        o = alpha * o_c if o is None else o + alpha * o_c
    ga = ga_ref[0].astype(F32)
    ya = (o * (ga * jax.nn.sigmoid(ga))).astype(BF16)

    mixed = (jnp.dot(yf, wo_ref[0:GROUP_W, :], preferred_element_type=F32)
             + jnp.dot(ya, wo_ref[GROUP_W:, :], preferred_element_type=F32))
    y_ref[0] = x_ref[0] + mixed


def _final(x, zr, zi, gf, ga, o_list, lse_list, a_blk, b_blk, w_out):
    B, S, D = x.shape
    tm = FINAL_TM
    d1, d2 = DILATIONS[1], DILATIONS[2]
    nat = lambda w: pl.BlockSpec((1, tm, w), lambda b, i: (b, i, 0))
    perm = lambda d, w: pl.BlockSpec((1, d, tm // d, w), lambda b, i: (b, 0, i, 0))
    const = lambda shp: pl.BlockSpec(shp, lambda b, i: (0,) * len(shp), pipeline_mode=pl.Buffered(1))
    p1 = jnp.asarray(_unpermute_matrix(PERM_ROWS, d1)).astype(BF16)
    p2 = jnp.asarray(_unpermute_matrix(PERM_ROWS, d2)).astype(BF16)
    ex = jnp.asarray(_slot_expand_matrix()).astype(BF16)
    return pl.pallas_call(
        _final_kernel,
        out_shape=jax.ShapeDtypeStruct((B, S, D), F32),
        grid=(B, S // tm),
        in_specs=[nat(D), nat(GROUP_W), nat(GROUP_W), nat(GROUP_W), nat(GROUP_W),
                  nat(GROUP_W), perm(d1, GROUP_W), perm(d2, GROUP_W), nat(128), perm(d1, 128), perm(d2, 128),
                  const((2, 256, 256)), const((2, 256, 256)), const((128, GROUP_W)),
                  const((PERM_ROWS, PERM_ROWS)), const((PERM_ROWS, PERM_ROWS)), const((D, D))],
        out_specs=nat(D),
        scratch_shapes=[pltpu.VMEM((tm, 128), F32), pltpu.VMEM((tm, 128), F32)],
        compiler_params=pltpu.CompilerParams(
            dimension_semantics=("arbitrary", "arbitrary"), vmem_limit_bytes=VMEM_LIMIT),
        name="final_mix",
    )(x, zr, zi, gf, ga, *o_list, *lse_list, a_blk, b_blk, ex, p1, p2, w_out)


def kernel(x, norm_w, w_in, q_norm_w, k_norm_w, w_fourier, w_out):
    B, S, D = x.shape
    n_cfg = len(DILATIONS)
    xf = x.astype(F32)
    gains = jnp.concatenate([q_norm_w.astype(F32).reshape(n_cfg, GROUP_W) * (HEAD_DIM ** -0.5 * LOG2E),
                             k_norm_w.astype(F32).reshape(n_cfg, GROUP_W)], axis=0)
    proj = _project(xf, norm_w.astype(F32), w_in.astype(BF16), gains)
    u_f, g_f, g_a = proj[:3]

    outs, lses = [], []
    for c, d in enumerate(DILATIONS):
        q, k, v = (t.reshape(B * d, S // d, GROUP_W) for t in proj[3 + 3 * c:6 + 3 * c])
        o, lse = _attention(q, k, v, d)
        if d == 1:
            outs.append(o)
            lses.append(lse)
        else:
            outs.append(o.reshape(B, d, S // d, GROUP_W))
            lses.append(lse.reshape(B, d, S // d, 128))

    zr, zi = _seq_dft(u_f)
    a_blk, b_blk = _fold_fourier_weights(w_fourier)
    y = _final(xf, zr, zi, g_f, g_a, outs, lses, a_blk, b_blk, w_out.astype(BF16))
    return y.astype(x.dtype)
```

```python
import functools

import numpy as np
import jax
import jax.numpy as jnp
from jax import lax
from jax.experimental import pallas as pl
from jax.experimental.pallas import tpu as pltpu

D_MODEL = 1024
HEAD_DIM = 64
N_SLOTS = 8
GROUP_W = N_SLOTS * HEAD_DIM
DILATIONS = (1, 4, 16)
RADIUS = 64
NORM_EPS = 1e-6
MASK_VALUE = -1e30
LOG2E = float(np.log2(np.e))
SEQ = 8192

NR, NJ = 8, 16

VMEM_LIMIT = 60 * 1024 * 1024

BF16 = jnp.bfloat16
F32 = jnp.float32


@functools.lru_cache(maxsize=None)
def _dft_stage_matrices():
    r = np.arange(NR, dtype=np.float64)
    j = np.arange(NJ, dtype=np.float64)
    eye_j = (j[:, None] == j[None, :]).astype(np.float64)[None, :, None, :]

    def radix_stage(phase):
        ph = 2 * np.pi * np.broadcast_to(phase, (NR, NJ, NR, NJ))
        c, s = np.cos(ph) * eye_j, np.sin(ph) * eye_j
        g = np.zeros((NR, 2, NJ, NR, 2, NJ))
        g[:, 0, :, :, 0, :] = c
        g[:, 0, :, :, 1, :] = s
        g[:, 1, :, :, 0, :] = -s
        g[:, 1, :, :, 1, :] = c
        return g.reshape(2 * NR * NJ, 2 * NR * NJ)

    kk = r[:, None, None, None]
    dig = r[None, None, :, None]
    jj = j[None, :, None, None]
    n_total = NJ * NR ** 3
    ga = np.stack([radix_stage(dig * kk / NR + bv * kk / NR ** 2 + jj * kk / n_total) for bv in range(NR)])
    gb = np.stack([radix_stage(dig * kk / NR + cv * kk / NR ** 2 + jj * kk / (NJ * NR ** 2)) for cv in range(NR)])
    gc = np.stack([radix_stage(dig * kk / NR + jj * kk / (NJ * NR) + dig * kav / NR ** 3) for kav in range(NR)])

    ph = 2 * np.pi * j[None, :] * j[:, None] / NJ
    c, s = np.cos(ph), np.sin(ph)
    gj = np.zeros((2, NJ, 2, NR, 2, NR, 2, NJ))
    for kb in range(2):
        for kav in range(NR):
            gj[0, :, kb, kav, kb, kav, 0, :] = c
            gj[0, :, kb, kav, kb, kav, 1, :] = s
            gj[1, :, kb, kav, kb, kav, 0, :] = -s
            gj[1, :, kb, kav, kb, kav, 1, :] = c
    gj = gj.reshape(4 * NR * NJ, 4 * NR * NJ)
    return tuple(np.asarray(g, dtype=np.float32) for g in (ga, gb, gc, gj))


@functools.lru_cache(maxsize=None)
def _mirror_matrix(rows):
    t = np.arange(rows)
    return np.asarray((t[:, None] + t[None, :] == rows) & (t[:, None] > 0), dtype=np.float32)


@functools.lru_cache(maxsize=None)
def _head_dim_dft():
    m = np.arange(HEAD_DIM, dtype=np.float64)
    ang = 2 * np.pi * m[:, None] * m[None, :] / HEAD_DIM
    scale = 1.0 / np.sqrt(SEQ * HEAD_DIM)
    return (np.asarray(np.cos(ang) * scale, dtype=np.float32), np.asarray(np.sin(ang) * scale, dtype=np.float32))


@functools.lru_cache(maxsize=None)
def _attn_bias(dilation):
    slopes = 2.0 ** (-8.0 * np.arange(1, N_SLOTS + 1, dtype=np.float64) / N_SLOTS)
    a = np.arange(2 * RADIUS)[:, None]
    c = np.arange(4 * RADIUS)[None, :]
    rel = np.abs(c - RADIUS - a)
    key_ok = (np.ones_like(c, dtype=bool), c >= RADIUS, c < 3 * RADIUS)
    tab = np.zeros((3, N_SLOTS, 2 * RADIUS, 4 * RADIUS))
    for var in range(3):
        for s in range(N_SLOTS):
            tab[var, s] = np.where((rel <= RADIUS) & key_ok[var], -slopes[s] * rel * dilation * LOG2E, MASK_VALUE)
    return np.asarray(tab.reshape(3, N_SLOTS // 2, 4 * RADIUS, 4 * RADIUS), dtype=np.float32)


@functools.lru_cache(maxsize=None)
def _slot_expand_matrix():
    r = np.arange(128)[:, None]
    c = np.arange(GROUP_W)[None, :]
    return np.asarray((r == c // HEAD_DIM), dtype=np.float32)


@functools.lru_cache(maxsize=None)
def _unpermute_matrix(tm, dilation):
    t = np.arange(tm)
    src = (t % dilation) * (tm // dilation) + t // dilation
    return np.asarray(src[:, None] == np.arange(tm)[None, :], dtype=np.float32)


PROJ_TM = 512
COL_UF, COL_GF, COL_Q, COL_K, COL_V, COL_GA = 0, 1, 2, 5, 8, 11


def _proj_kernel(x_ref, nw_ref, w_ref, gain_ref, uf_ref, gf_ref, ga_ref, *rest):
    qkv_refs, hn_ref = rest[:-1], rest[-1]
    x = x_ref[0]
    ms = jnp.mean(x * x, axis=-1, keepdims=True)
    hn32 = x * lax.rsqrt(ms + NORM_EPS) * nw_ref[...]
    n_slab = hn_ref.shape[0]
    for s in range(n_slab):
        hn_ref[s] = hn32[:, s * 128:(s + 1) * 128]
    low = lax.broadcasted_iota(jnp.int32, (PROJ_TM, 128), 1) < HEAD_DIM

    def project(hn, col, gain_row):
        acc = jnp.dot(hn, w_ref[:, col * GROUP_W:(col + 1) * GROUP_W], preferred_element_type=F32)
        if gain_row is None:
            return acc.astype(BF16)
        tiles = []
        for t in range(GROUP_W // 128):
            a = acc[:, t * 128:(t + 1) * 128]
            z = a * a
            z_lo = jnp.where(low, z, 0.0)
            ms_lo = jnp.sum(z_lo, axis=-1, keepdims=True) * (1.0 / HEAD_DIM)
            ms_hi = jnp.sum(z - z_lo, axis=-1, keepdims=True) * (1.0 / HEAD_DIM)
            scale = jnp.where(low, lax.rsqrt(ms_lo + NORM_EPS), lax.rsqrt(ms_hi + NORM_EPS))
            tiles.append(a * scale)
        return (jnp.concatenate(tiles, axis=-1) * gain_ref[gain_row:gain_row + 1, :]).astype(BF16)

    hn = hn32.astype(BF16)
    uf_ref[0] = project(hn, COL_UF, None)
    gf_ref[0] = project(hn, COL_GF, None)
    ga_ref[0] = project(hn, COL_GA, None)
    for c, d in enumerate(DILATIONS):
        q_ref, k_ref, v_ref = qkv_refs[3 * c:3 * c + 3]
        rows = PROJ_TM // d
        if d > 1:
            hn = jnp.concatenate(
                [jnp.concatenate([hn_ref[s, pl.ds(r, rows, stride=d), :] for s in range(n_slab)], axis=1)
                 for r in range(d)], axis=0).astype(BF16)
        for o_ref, col, gain_row in ((q_ref, COL_Q + c, c), (k_ref, COL_K + c, 3 + c), (v_ref, COL_V + c, None)):
            res = project(hn, col, gain_row)
            for r in range(d):
                o_ref[0, r] = res[r * rows:(r + 1) * rows]


def _project(x, norm_w, w_in, gains):
    B, S, D = x.shape
    tm = PROJ_TM
    nat = jax.ShapeDtypeStruct((B, S, GROUP_W), BF16)
    nat_spec = pl.BlockSpec((1, tm, GROUP_W), lambda b, i: (b, i, 0))
    out_shape, out_specs = [nat] * 3, [nat_spec] * 3
    for d in DILATIONS:
        out_shape += [jax.ShapeDtypeStruct((B, d, S // d, GROUP_W), BF16)] * 3
        out_specs += [pl.BlockSpec((1, d, tm // d, GROUP_W), lambda b, i: (b, 0, i, 0))] * 3
    const = lambda shp: pl.BlockSpec(shp, lambda b, i: (0,) * len(shp), pipeline_mode=pl.Buffered(1))
    return pl.pallas_call(
        _proj_kernel,
        out_shape=out_shape,
        grid=(B, S // tm),
        in_specs=[pl.BlockSpec((1, tm, D), lambda b, i: (b, i, 0)),
                  const((1, D)), const(w_in.shape), const(gains.shape)],
        out_specs=out_specs,
        scratch_shapes=[pltpu.VMEM((D // 128, tm, 128), F32)],
        compiler_params=pltpu.CompilerParams(
            dimension_semantics=("arbitrary", "arbitrary"), vmem_limit_bytes=VMEM_LIMIT),
        name="proj",
    )(x, norm_w.reshape(1, D), w_in, gains)


ATT_STEP_ROWS = 4096
ATT_SUB = 2 * RADIUS


def _attn_kernel(q_ref, kp_ref, kc_ref, kn_ref, vp_ref, vc_ref, vn_ref, bias_ref, o_ref, st_ref, kbuf, vbuf):
    nb, ti = q_ref.shape[0], q_ref.shape[1]
    i = pl.program_id(1)
    first = i == 0
    last = i == pl.num_programs(1) - 1
    lane = lax.broadcasted_iota(jnp.int32, (ATT_SUB, 128), 1)
    low = lane < HEAD_DIM
    ones = jnp.ones((2 * ATT_SUB, 128), BF16)
    n_sub = ti // ATT_SUB
    for b in range(nb):
        kbuf[b, 0:RADIUS] = kp_ref[b]
        kbuf[b, RADIUS:RADIUS + ti] = kc_ref[b]
        kbuf[b, RADIUS + ti:] = kn_ref[b]
        vbuf[b, 0:RADIUS] = vp_ref[b]
        vbuf[b, RADIUS:RADIUS + ti] = vc_ref[b]
        vbuf[b, RADIUS + ti:] = vn_ref[b]
        for j in range(n_sub):
            r0 = j * ATT_SUB
            variant = 0
            if j == 0:
                variant = jnp.where(first, 1, 0)
            if j == n_sub - 1:
                variant = jnp.where(last, 2, 0)
            st = jnp.zeros((ATT_SUB, 128), F32)
            for p in range(N_SLOTS // 2):
                c0 = p * 128
                qp = q_ref[b, r0:r0 + ATT_SUB, c0:c0 + 128]
                kw = kbuf[b, r0:r0 + 2 * ATT_SUB, c0:c0 + 128]
                vw = jnp.concatenate([vbuf[b, r0:r0 + 2 * ATT_SUB, c0:c0 + 128], ones], axis=1)
                zero = jnp.zeros_like(qp)
                qs = jnp.concatenate([jnp.where(low, qp, zero), jnp.where(low, zero, qp)], axis=0)
                s = lax.dot_general(qs, kw, (((1,), (1,)), ((), ())), preferred_element_type=F32)
                s = s + bias_ref[variant, p]
                m = jnp.max(s, axis=-1, keepdims=True)
                e = jnp.exp2(s - m).astype(BF16)
                ol = jnp.dot(e, vw, preferred_element_type=F32)
                o_ref[b, r0:r0 + ATT_SUB, c0:c0 + 128] = jnp.where(
                    low, ol[:ATT_SUB, :128], ol[ATT_SUB:, :128]).astype(BF16)
                st = jnp.where(lane == 2 * p, m[:ATT_SUB], st)
                st = jnp.where(lane == 2 * p + 1, m[ATT_SUB:], st)
                st = jnp.where(lane == N_SLOTS + 2 * p, ol[:ATT_SUB, 128:], st)
                st = jnp.where(lane == N_SLOTS + 2 * p + 1, ol[ATT_SUB:, 128:], st)
            st_ref[b, r0:r0 + ATT_SUB, :] = st


def _attention(q, k, v, dilation):
    N, L, W = q.shape
    ti = min(L, ATT_STEP_ROWS)
    nb = ATT_STEP_ROWS // ti
    nblk = L // ti
    halo_per_blk = ti // RADIUS
    n_halo = L // RADIUS
    bias = jnp.asarray(_attn_bias(dilation))
    cur = pl.BlockSpec((nb, ti, W), lambda n, i: (n, i, 0))
    prev = pl.BlockSpec((nb, RADIUS, W), lambda n, i: (n, jnp.maximum(i * halo_per_blk - 1, 0), 0))
    nxt = pl.BlockSpec((nb, RADIUS, W), lambda n, i: (n, jnp.minimum((i + 1) * halo_per_blk, n_halo - 1), 0))
    return pl.pallas_call(
        _attn_kernel,
        out_shape=[jax.ShapeDtypeStruct((N, L, W), BF16), jax.ShapeDtypeStruct((N, L, 128), F32)],
        grid=(N // nb, nblk),
        in_specs=[cur, prev, cur, nxt, prev, cur, nxt,
                  pl.BlockSpec(bias.shape, lambda n, i: (0, 0, 0, 0), pipeline_mode=pl.Buffered(1))],
        out_specs=[pl.BlockSpec((nb, ti, W), lambda n, i: (n, i, 0)),
                   pl.BlockSpec((nb, ti, 128), lambda n, i: (n, i, 0))],
        scratch_shapes=[pltpu.VMEM((nb, ti + 2 * RADIUS, W), BF16), pltpu.VMEM((nb, ti + 2 * RADIUS, W), BF16)],
        compiler_params=pltpu.CompilerParams(
            dimension_semantics=("arbitrary", "arbitrary"), vmem_limit_bytes=VMEM_LIMIT),
        name=f"attn_d{dilation}",
    )(q, k, k, k, v, v, v, bias)


FFT_CW = GROUP_W // 2
FFT_UNROLL = 32


def _fft_kernel(u_ref, ga_ref, gb_ref, gc_ref, gj_ref, z_ref, y_ref, y2_ref):
    cw = FFT_CW
    grp = NR * 2 * NJ

    def stage_a(g, carry):
        b, c = g // NR, g % NR
        u = u_ref[0, :, pl.ds(b, 1), pl.ds(c, 1), :, :].reshape(NR, NJ, 2 * cw)
        xg = jnp.concatenate([u[:, :, :cw], u[:, :, cw:]], axis=1).reshape(grp, cw)
        r = jnp.dot(ga_ref[b], xg, preferred_element_type=F32)
        y_ref[pl.ds(b, 1), pl.ds(c, 1)] = r.astype(BF16).reshape(1, 1, NR, 2, NJ, cw)
        return carry

    lax.fori_loop(0, NR * NR, stage_a, 0, unroll=FFT_UNROLL)

    def stage_b(g, carry):
        c, ka = g // NR, g % NR
        xg = y_ref[:, pl.ds(c, 1), pl.ds(ka, 1), :, :, :].reshape(grp, cw)
        r = jnp.dot(gb_ref[c], xg, preferred_element_type=F32)
        y2_ref[:, pl.ds(c, 1), pl.ds(ka, 1), :, :, :] = r.astype(BF16).reshape(NR, 1, 1, 2, NJ, cw)
        return carry

    lax.fori_loop(0, NR * NR, stage_b, 0, unroll=FFT_UNROLL)

    def stage_c(g, carry):
        kb, ka = g // NR, g % NR
        xg = y2_ref[pl.ds(kb, 1), :, pl.ds(ka, 1), :, :, :].reshape(grp, cw)
        r = jnp.dot(gc_ref[ka], xg, preferred_element_type=F32)
        y_ref[pl.ds(kb, 1), :, pl.ds(ka, 1), :, :, :] = r.astype(BF16).reshape(1, NR, 1, 2, NJ, cw)
        return carry

    lax.fori_loop(0, NR * NR, stage_c, 0, unroll=FFT_UNROLL)

    def stage_j(g, carry):
        m, kc = g // NR, g % NR
        xg = y_ref[pl.ds(2 * m, 2), pl.ds(kc, 1)].reshape(2 * grp, cw)
        r = jnp.dot(gj_ref[...], xg, preferred_element_type=F32).astype(BF16)
        z_ref[0, :, pl.ds(kc, 1), pl.ds(m, 1), :, 0:cw] = r[:grp].reshape(NJ, 1, 1, 2 * NR, cw)
        z_ref[0, :, pl.ds(kc, 1), pl.ds(m, 1), :, cw:] = r[grp:].reshape(NJ, 1, 1, 2 * NR, cw)
        return carry

    lax.fori_loop(0, NR * NR // 2, stage_j, 0, unroll=FFT_UNROLL // 2)


def _seq_dft(u):
    B, S, W = u.shape
    mats = [jnp.asarray(g).astype(BF16) for g in _dft_stage_matrices()]
    u6 = u.reshape(B, NR, NR, NR, NJ, W)
    const = lambda shp: pl.BlockSpec(shp, lambda b: (0,) * len(shp), pipeline_mode=pl.Buffered(1))
    z = pl.pallas_call(
        _fft_kernel,
        out_shape=jax.ShapeDtypeStruct((B, NJ, NR, NR // 2, 2 * NR, W), BF16),
        grid=(B,),
        in_specs=[pl.BlockSpec((1, NR, NR, NR, NJ, W), lambda b: (b, 0, 0, 0, 0, 0))] + [const(m.shape) for m in mats],
        out_specs=pl.BlockSpec((1, NJ, NR, NR // 2, 2 * NR, W), lambda b: (b, 0, 0, 0, 0, 0)),
        scratch_shapes=[pltpu.VMEM((NR, NR, NR, 2, NJ, FFT_CW), BF16)] * 2,
        compiler_params=pltpu.CompilerParams(dimension_semantics=("arbitrary",), vmem_limit_bytes=VMEM_LIMIT),
        name="seq_dft",
    )(u6, *mats)
    return z.reshape(B, S, W)


def _fold_kernel(cd_ref, sd_ref, wf_ref, m_ref, n_ref):
    m_ref[...] = jnp.zeros(m_ref.shape, m_ref.dtype)
    n_ref[...] = jnp.zeros(n_ref.shape, n_ref.dtype)
    for g in range(N_SLOTS):
        h, o = divmod(g, N_SLOTS // 2)
        w = wf_ref[g]
        blk = slice(o * HEAD_DIM, (o + 1) * HEAD_DIM)
        m_ref[h, blk, blk] = (0.5 * jnp.dot(cd_ref[...], w, preferred_element_type=F32,
                                            precision=lax.Precision.HIGHEST)).astype(BF16)
        n_ref[h, blk, blk] = (0.5 * jnp.dot(sd_ref[...], w, preferred_element_type=F32,
                                            precision=lax.Precision.HIGHEST)).astype(BF16)


def _fold_fourier_weights(w_fourier):
    cd, sd = _head_dim_dft()
    shp = jax.ShapeDtypeStruct((2, FFT_CW, FFT_CW), BF16)
    return pl.pallas_call(_fold_kernel, out_shape=[shp, shp], name="fold_fourier")(
        jnp.asarray(cd), jnp.asarray(sd), w_fourier.astype(F32))


FINAL_TM = 1024
PERM_ROWS = 256


def _final_kernel(x_ref, z_ref, zm_ref, zfix_ref, gf_ref, ga_ref, o0_ref, o1_ref, o2_ref, l0_ref, l1_ref, l2_ref,
                  w1_ref, w2_ref, ex_ref, p1_ref, p2_ref, j1_ref, wo_ref, y_ref, l1_nat, l2_nat):
    tm = FINAL_TM
    n_sub = tm // PERM_ROWS
    cw = FFT_CW
    first_row = lax.broadcasted_iota(jnp.int32, (PERM_ROWS, GROUP_W), 0) == 0
    fs = []
    for s in range(n_sub):
        own = z_ref[0, s * PERM_ROWS:(s + 1) * PERM_ROWS, :].astype(F32)
        oth = zm_ref[0, (n_sub - 1 - s) * PERM_ROWS:(n_sub - s) * PERM_ROWS, :]
        g = pl.program_id(1) * n_sub + s
        fix = zfix_ref[0, pl.ds(g, 1), :]
        zm = jnp.where(first_row, fix, jnp.dot(j1_ref[...], oth, preferred_element_type=F32))
        zr, zi, mr, mi = own[:, :cw], own[:, cw:], zm[:, :cw], zm[:, cw:]
        parts = (((zr + mr), (zi - mi)), ((zi + mi), (mr - zr)))
        fs.append(jnp.concatenate(
            [jnp.dot(re.astype(BF16), w1_ref[h], preferred_element_type=F32)
             + jnp.dot(im.astype(BF16), w2_ref[h], preferred_element_type=F32)
             for h, (re, im) in enumerate(parts)], axis=-1))
    f = jnp.concatenate(fs, axis=0)
    gf = gf_ref[0].astype(F32)
    yf = (f * (gf * jax.nn.sigmoid(gf))).astype(BF16)

    for l_ref, l_nat, d in ((l1_ref, l1_nat, DILATIONS[1]), (l2_ref, l2_nat, DILATIONS[2])):
        for r in range(d):
            l_nat[pl.ds(r, tm // d, stride=d), :] = l_ref[0, r]
    o0 = o0_ref[0].astype(F32)

    def unpermute(o_ref, p_ref, d):
        rows = PERM_ROWS // d
        parts = []
        for k in range(n_sub):
            src = jnp.concatenate([o_ref[0, r, k * rows:(k + 1) * rows, :] for r in range(d)], axis=0)
            parts.append(jnp.dot(p_ref[...], src, preferred_element_type=F32))
        return jnp.concatenate(parts, axis=0)

    o1 = unpermute(o1_ref, p1_ref, DILATIONS[1])
    o2 = unpermute(o2_ref, p2_ref, DILATIONS[2])

    sts = (l0_ref[0], l1_nat[...], l2_nat[...])
    mx = jnp.maximum(jnp.maximum(sts[0], sts[1]), sts[2])
    es = [jnp.exp2(st - mx) for st in sts]
    den = None
    for e_c, st in zip(es, sts):
        term = e_c * pltpu.roll(st, 128 - N_SLOTS, 1)
        den = term if den is None else den + term
    slot_lane = lax.broadcasted_iota(jnp.int32, den.shape, 1) < N_SLOTS
    inv = 1.0 / jnp.where(slot_lane, den, 1.0)
    ex = ex_ref[...]
    o = None
    for e_c, o_c in zip(es, (o0, o1, o2)):
        alpha = jnp.dot((e_c * inv).astype(BF16), ex, preferred_element_type=F32)
        o = alpha * o_c if o is None else o + alpha * o_c
    ga = ga_ref[0].astype(F32)
    ya = (o * (ga * jax.nn.sigmoid(ga))).astype(BF16)

    mixed = (jnp.dot(yf, wo_ref[0:GROUP_W, :], preferred_element_type=F32)
             + jnp.dot(ya, wo_ref[GROUP_W:, :], preferred_element_type=F32))
    y_ref[0] = x_ref[0] + mixed


def _final(x, z, gf, ga, o_list, lse_list, w1, w2, w_out):
    B, S, D = x.shape
    tm = FINAL_TM
    nblk = S // tm
    d1, d2 = DILATIONS[1], DILATIONS[2]
    zfix = jnp.roll(z[:, ::PERM_ROWS, :][:, ::-1, :], 1, axis=1).astype(F32)
    nat = lambda w: pl.BlockSpec((1, tm, w), lambda b, i: (b, i, 0))
    perm = lambda d, w: pl.BlockSpec((1, d, tm // d, w), lambda b, i: (b, 0, i, 0))
    const = lambda shp: pl.BlockSpec(shp, lambda b, i: (0,) * len(shp), pipeline_mode=pl.Buffered(1))
    p1 = jnp.asarray(_unpermute_matrix(PERM_ROWS, d1)).astype(BF16)
    p2 = jnp.asarray(_unpermute_matrix(PERM_ROWS, d2)).astype(BF16)
    j1 = jnp.asarray(_mirror_matrix(PERM_ROWS)).astype(BF16)
    ex = jnp.asarray(_slot_expand_matrix()).astype(BF16)
    return pl.pallas_call(
        _final_kernel,
        out_shape=jax.ShapeDtypeStruct((B, S, D), F32),
        grid=(B, nblk),
        in_specs=[nat(D), nat(GROUP_W), pl.BlockSpec((1, tm, GROUP_W), lambda b, i: (b, nblk - 1 - i, 0)),
                  pl.BlockSpec((1, S // PERM_ROWS, GROUP_W), lambda b, i: (b, 0, 0)),
                  nat(GROUP_W), nat(GROUP_W),
                  nat(GROUP_W), perm(d1, GROUP_W), perm(d2, GROUP_W), nat(128), perm(d1, 128), perm(d2, 128),
                  const((2, FFT_CW, FFT_CW)), const((2, FFT_CW, FFT_CW)), const((128, GROUP_W)),
                  const((PERM_ROWS, PERM_ROWS)), const((PERM_ROWS, PERM_ROWS)), const((PERM_ROWS, PERM_ROWS)),
                  const((D, D))],
        out_specs=nat(D),
        scratch_shapes=[pltpu.VMEM((tm, 128), F32), pltpu.VMEM((tm, 128), F32)],
        compiler_params=pltpu.CompilerParams(
            dimension_semantics=("arbitrary", "arbitrary"), vmem_limit_bytes=VMEM_LIMIT),
        name="final_mix",
    )(x, z, z, zfix, gf, ga, *o_list, *lse_list, w1, w2, ex, p1, p2, j1, w_out)


def kernel(x, norm_w, w_in, q_norm_w, k_norm_w, w_fourier, w_out):
    B, S, D = x.shape
    n_cfg = len(DILATIONS)
    xf = x.astype(F32)
    gains = jnp.concatenate([q_norm_w.astype(F32).reshape(n_cfg, GROUP_W) * (HEAD_DIM ** -0.5 * LOG2E),
                             k_norm_w.astype(F32).reshape(n_cfg, GROUP_W)], axis=0)
    proj = _project(xf, norm_w.astype(F32), w_in.astype(BF16), gains)
    u_f, g_f, g_a = proj[:3]

    outs, lses = [], []
    for c, d in enumerate(DILATIONS):
        q, k, v = (t.reshape(B * d, S // d, GROUP_W) for t in proj[3 + 3 * c:6 + 3 * c])
        o, lse = _attention(q, k, v, d)
        if d == 1:
            outs.append(o)
            lses.append(lse)
        else:
            outs.append(o.reshape(B, d, S // d, GROUP_W))
            lses.append(lse.reshape(B, d, S // d, 128))

    z = _seq_dft(u_f)
    w1, w2 = _fold_fourier_weights(w_fourier)
    y = _final(xf, z, g_f, g_a, outs, lses, w1, w2, w_out.astype(BF16))
    return y.astype(x.dtype)
```

```python
import functools

import numpy as np
import jax
import jax.numpy as jnp
from jax import lax
from jax.experimental import pallas as pl
from jax.experimental.pallas import tpu as pltpu

D_MODEL = 1024
HEAD_DIM = 64
N_SLOTS = 8
GROUP_W = N_SLOTS * HEAD_DIM
DILATIONS = (1, 4, 16)
RADIUS = 64
NORM_EPS = 1e-6
MASK_VALUE = -1e30
LOG2E = float(np.log2(np.e))
SEQ = 8192

NR, NJ = 8, 16

LANES = 128
V7X_VMEM_BYTES = 64 * 1024 * 1024
VMEM_LIMIT = V7X_VMEM_BYTES - 4 * 1024 * 1024

BF16 = jnp.bfloat16
F32 = jnp.float32


@functools.lru_cache(maxsize=None)
def _dft_stage_matrices():
    r = np.arange(NR, dtype=np.float64)
    j = np.arange(NJ, dtype=np.float64)
    eye_j = (j[:, None] == j[None, :]).astype(np.float64)[None, :, None, :]

    def radix_stage(phase):
        ph = 2 * np.pi * np.broadcast_to(phase, (NR, NJ, NR, NJ))
        c, s = np.cos(ph) * eye_j, np.sin(ph) * eye_j
        g = np.zeros((NR, 2, NJ, NR, 2, NJ))
        g[:, 0, :, :, 0, :] = c
        g[:, 0, :, :, 1, :] = s
        g[:, 1, :, :, 0, :] = -s
        g[:, 1, :, :, 1, :] = c
        return g.reshape(2 * NR * NJ, 2 * NR * NJ)

    kk = r[:, None, None, None]
    dig = r[None, None, :, None]
    jj = j[None, :, None, None]
    n_total = NJ * NR ** 3
    ga = np.stack([radix_stage(dig * kk / NR + bv * kk / NR ** 2 + jj * kk / n_total) for bv in range(NR)])
    gb = np.stack([radix_stage(dig * kk / NR + cv * kk / NR ** 2 + jj * kk / (NJ * NR ** 2)) for cv in range(NR)])
    gc = np.stack([radix_stage(dig * kk / NR + jj * kk / (NJ * NR) + dig * kav / NR ** 3) for kav in range(NR)])

    ph = 2 * np.pi * j[None, :] * j[:, None] / NJ
    c, s = np.cos(ph), np.sin(ph)
    gj = np.zeros((2, NJ, 2, NR, 2, NR, 2, NJ))
    for kb in range(2):
        for kav in range(NR):
            gj[0, :, kb, kav, kb, kav, 0, :] = c
            gj[0, :, kb, kav, kb, kav, 1, :] = s
            gj[1, :, kb, kav, kb, kav, 0, :] = -s
            gj[1, :, kb, kav, kb, kav, 1, :] = c
    gj = gj.reshape(4 * NR * NJ, 4 * NR * NJ)
    return tuple(np.asarray(g, dtype=np.float32) for g in (ga, gb, gc, gj))


@functools.lru_cache(maxsize=None)
def _mirror_matrix(rows):
    t = np.arange(rows)
    return np.asarray((t[:, None] + t[None, :] == rows) & (t[:, None] > 0), dtype=np.float32)


@functools.lru_cache(maxsize=None)
def _head_dim_dft():
    m = np.arange(HEAD_DIM, dtype=np.float64)
    ang = 2 * np.pi * m[:, None] * m[None, :] / HEAD_DIM
    scale = 1.0 / np.sqrt(SEQ * HEAD_DIM)
    return (np.asarray(np.cos(ang) * scale, dtype=np.float32), np.asarray(np.sin(ang) * scale, dtype=np.float32))


@functools.lru_cache(maxsize=None)
def _attn_bias(dilation):
    slopes = 2.0 ** (-8.0 * np.arange(1, N_SLOTS + 1, dtype=np.float64) / N_SLOTS)
    a = np.arange(2 * RADIUS)[:, None]
    c = np.arange(4 * RADIUS)[None, :]
    rel = np.abs(c - RADIUS - a)
    key_ok = (np.ones_like(c, dtype=bool), c >= RADIUS, c < 3 * RADIUS)
    tab = np.zeros((3, N_SLOTS, 2 * RADIUS, 4 * RADIUS))
    for var in range(3):
        for s in range(N_SLOTS):
            tab[var, s] = np.where((rel <= RADIUS) & key_ok[var], -slopes[s] * rel * dilation * LOG2E, MASK_VALUE)
    return np.asarray(tab.reshape(3, N_SLOTS // 2, 4 * RADIUS, 4 * RADIUS), dtype=np.float32)


@functools.lru_cache(maxsize=None)
def _slot_expand_matrix():
    r = np.arange(LANES)[:, None]
    c = np.arange(GROUP_W)[None, :]
    return np.asarray((r == c // HEAD_DIM), dtype=np.float32)


@functools.lru_cache(maxsize=None)
def _unpermute_matrix(tm, dilation):
    t = np.arange(tm)
    src = (t % dilation) * (tm // dilation) + t // dilation
    return np.asarray(src[:, None] == np.arange(tm)[None, :], dtype=np.float32)


PROJ_TM = 512
COL_UF, COL_GF, COL_Q, COL_K, COL_V, COL_GA = 0, 1, 2, 5, 8, 11


def _proj_kernel(x_ref, nw_ref, w_ref, gain_ref, uf_ref, gf_ref, ga_ref, *rest):
    qkv_refs, xw_ref, rs_ref = rest[:-2], rest[-2], rest[-1]
    x = x_ref[0]
    xw = x * nw_ref[...]
    ms = jnp.mean(x * x, axis=-1, keepdims=True) + NORM_EPS
    n_slab = xw_ref.shape[0]
    for s in range(n_slab):
        xw_ref[s] = xw[:, s * LANES:(s + 1) * LANES]
    rs_ref[0] = jnp.broadcast_to(lax.rsqrt(ms), (PROJ_TM, LANES))
    rs_ref[1] = jnp.broadcast_to(NORM_EPS * ms, (PROJ_TM, LANES))
    low = lax.broadcasted_iota(jnp.int32, (PROJ_TM, LANES), 1) < HEAD_DIM

    def project(lhs, row_scale, head_eps, col, gain_row):
        acc = jnp.dot(lhs, w_ref[:, col * GROUP_W:(col + 1) * GROUP_W], preferred_element_type=F32)
        tiles = []
        for t in range(GROUP_W // LANES):
            a = acc[:, t * LANES:(t + 1) * LANES]
            if gain_row is None:
                tiles.append(a * row_scale)
                continue
            z = a * a
            z_lo = jnp.where(low, z, 0.0)
            ms_lo = jnp.sum(z_lo, axis=-1, keepdims=True) * (1.0 / HEAD_DIM)
            ms_hi = jnp.sum(z - z_lo, axis=-1, keepdims=True) * (1.0 / HEAD_DIM)
            tiles.append(a * lax.rsqrt(jnp.where(low, ms_lo, ms_hi) + head_eps))
        out = jnp.concatenate(tiles, axis=-1)
        if gain_row is not None:
            out = out * gain_ref[gain_row:gain_row + 1, :]
        return out.astype(BF16)

    def regrouped(ref, idx, d):
        rows = PROJ_TM // d
        return jnp.concatenate([ref[idx, pl.ds(r, rows, stride=d), :] for r in range(d)], axis=0)

    lhs = xw.astype(BF16)
    row_scale, head_eps = rs_ref[0], rs_ref[1]
    uf_ref[0] = project(lhs, row_scale, head_eps, COL_UF, None)
    gf_ref[0] = project(lhs, row_scale, head_eps, COL_GF, None)
    ga_ref[0] = project(lhs, row_scale, head_eps, COL_GA, None)
    for c, d in enumerate(DILATIONS):
        q_ref, k_ref, v_ref = qkv_refs[3 * c:3 * c + 3]
        rows = PROJ_TM // d
        if d > 1:
            lhs = jnp.concatenate([regrouped(xw_ref, s, d) for s in range(n_slab)], axis=1).astype(BF16)
            row_scale, head_eps = regrouped(rs_ref, 0, d), regrouped(rs_ref, 1, d)
        for o_ref, col, gain_row in ((q_ref, COL_Q + c, c), (k_ref, COL_K + c, 3 + c), (v_ref, COL_V + c, None)):
            res = project(lhs, row_scale, head_eps, col, gain_row)
            for r in range(d):
                o_ref[0, r] = res[r * rows:(r + 1) * rows]


def _project(x, norm_w, w_in, gains):
    B, S, D = x.shape
    tm = PROJ_TM
    nat = jax.ShapeDtypeStruct((B, S, GROUP_W), BF16)
    nat_spec = pl.BlockSpec((1, tm, GROUP_W), lambda b, i: (b, i, 0))
    out_shape, out_specs = [nat] * 3, [nat_spec] * 3
    for d in DILATIONS:
        out_shape += [jax.ShapeDtypeStruct((B, d, S // d, GROUP_W), BF16)] * 3
        out_specs += [pl.BlockSpec((1, d, tm // d, GROUP_W), lambda b, i: (b, 0, i, 0))] * 3
    const = lambda shp: pl.BlockSpec(shp, lambda b, i: (0,) * len(shp), pipeline_mode=pl.Buffered(1))
    return pl.pallas_call(
        _proj_kernel,
        out_shape=out_shape,
        grid=(B, S // tm),
        in_specs=[pl.BlockSpec((1, tm, D), lambda b, i: (b, i, 0)),
                  const((1, D)), const(w_in.shape), const(gains.shape)],
        out_specs=out_specs,
        scratch_shapes=[pltpu.VMEM((D // LANES, tm, LANES), F32), pltpu.VMEM((2, tm, LANES), F32)],
        compiler_params=pltpu.CompilerParams(
            dimension_semantics=("arbitrary", "arbitrary"), vmem_limit_bytes=VMEM_LIMIT),
        name="proj",
    )(x, norm_w.reshape(1, D), w_in, gains)


ATT_STEP_ROWS = 4096
ATT_SUB = 2 * RADIUS


def _attn_kernel(q_ref, kp_ref, kc_ref, kn_ref, vp_ref, vc_ref, vn_ref, bias_ref, o_ref, st_ref, kbuf, vbuf):
    nb, ti = q_ref.shape[0], q_ref.shape[1]
    i = pl.program_id(1)
    first = i == 0
    last = i == pl.num_programs(1) - 1
    lane = lax.broadcasted_iota(jnp.int32, (ATT_SUB, LANES), 1)
    low = lane < HEAD_DIM
    ones = jnp.ones((2 * ATT_SUB, LANES), BF16)
    n_sub = ti // ATT_SUB
    for b in range(nb):
        kbuf[b, 0:RADIUS] = kp_ref[b]
        kbuf[b, RADIUS:RADIUS + ti] = kc_ref[b]
        kbuf[b, RADIUS + ti:] = kn_ref[b]
        vbuf[b, 0:RADIUS] = vp_ref[b]
        vbuf[b, RADIUS:RADIUS + ti] = vc_ref[b]
        vbuf[b, RADIUS + ti:] = vn_ref[b]
        for j in range(n_sub):
            r0 = j * ATT_SUB
            variant = 0
            if j == 0:
                variant = jnp.where(first, 1, 0)
            if j == n_sub - 1:
                variant = jnp.where(last, 2, 0)
            st = jnp.zeros((ATT_SUB, LANES), F32)
            for p in range(N_SLOTS // 2):
                c0 = p * LANES
                qp = q_ref[b, r0:r0 + ATT_SUB, c0:c0 + LANES]
                kw = kbuf[b, r0:r0 + 2 * ATT_SUB, c0:c0 + LANES]
                vw = jnp.concatenate([vbuf[b, r0:r0 + 2 * ATT_SUB, c0:c0 + LANES], ones], axis=1)
                zero = jnp.zeros_like(qp)
                qs = jnp.concatenate([jnp.where(low, qp, zero), jnp.where(low, zero, qp)], axis=0)
                s = lax.dot_general(qs, kw, (((1,), (1,)), ((), ())), preferred_element_type=F32)
                s = s + bias_ref[variant, p]
                m = jnp.max(s, axis=-1, keepdims=True)
                e = jnp.exp2(s - m).astype(BF16)
                ol = jnp.dot(e, vw, preferred_element_type=F32)
                o_ref[b, r0:r0 + ATT_SUB, c0:c0 + LANES] = jnp.where(
                    low, ol[:ATT_SUB, :LANES], ol[ATT_SUB:, :LANES]).astype(BF16)
                st = jnp.where(lane == 2 * p, m[:ATT_SUB], st)
                st = jnp.where(lane == 2 * p + 1, m[ATT_SUB:], st)
                st = jnp.where(lane == N_SLOTS + 2 * p, ol[:ATT_SUB, LANES:], st)
                st = jnp.where(lane == N_SLOTS + 2 * p + 1, ol[ATT_SUB:, LANES:], st)
            st_ref[b, r0:r0 + ATT_SUB, :] = st


def _attention(q, k, v, dilation):
    N, L, W = q.shape
    ti = min(L, ATT_STEP_ROWS)
    nb = ATT_STEP_ROWS // ti
    nblk = L // ti
    halo_per_blk = ti // RADIUS
    n_halo = L // RADIUS
    bias = jnp.asarray(_attn_bias(dilation))
    cur = pl.BlockSpec((nb, ti, W), lambda n, i: (n, i, 0))
    prev = pl.BlockSpec((nb, RADIUS, W), lambda n, i: (n, jnp.maximum(i * halo_per_blk - 1, 0), 0))
    nxt = pl.BlockSpec((nb, RADIUS, W), lambda n, i: (n, jnp.minimum((i + 1) * halo_per_blk, n_halo - 1), 0))
    return pl.pallas_call(
        _attn_kernel,
        out_shape=[jax.ShapeDtypeStruct((N, L, W), BF16), jax.ShapeDtypeStruct((N, L, LANES), F32)],
        grid=(N // nb, nblk),
        in_specs=[cur, prev, cur, nxt, prev, cur, nxt,
                  pl.BlockSpec(bias.shape, lambda n, i: (0, 0, 0, 0), pipeline_mode=pl.Buffered(1))],
        out_specs=[pl.BlockSpec((nb, ti, W), lambda n, i: (n, i, 0)),
                   pl.BlockSpec((nb, ti, LANES), lambda n, i: (n, i, 0))],
        scratch_shapes=[pltpu.VMEM((nb, ti + 2 * RADIUS, W), BF16), pltpu.VMEM((nb, ti + 2 * RADIUS, W), BF16)],
        compiler_params=pltpu.CompilerParams(
            dimension_semantics=("arbitrary", "arbitrary"), vmem_limit_bytes=VMEM_LIMIT),
        name=f"attn_d{dilation}",
    )(q, k, k, k, v, v, v, bias)


FFT_CW = GROUP_W // 2
FFT_UNROLL = NR * NR


def _fft_kernel(u_ref, ga_ref, gb_ref, gc_ref, gj_ref, z_ref, y_ref, y2_ref):
    cw = FFT_CW
    grp = NR * 2 * NJ

    def stage_a(g, carry):
        b, c = g // NR, g % NR
        u = u_ref[0, :, pl.ds(b, 1), pl.ds(c, 1), :, :].reshape(NR, NJ, 2 * cw)
        xg = jnp.concatenate([u[:, :, :cw], u[:, :, cw:]], axis=1).reshape(grp, cw)
        r = jnp.dot(ga_ref[b], xg, preferred_element_type=F32)
        y_ref[pl.ds(b, 1), pl.ds(c, 1)] = r.astype(BF16).reshape(1, 1, NR, 2, NJ, cw)
        return carry

    lax.fori_loop(0, NR * NR, stage_a, 0, unroll=FFT_UNROLL)

    def stage_b(g, carry):
        c, ka = g // NR, g % NR
        xg = y_ref[:, pl.ds(c, 1), pl.ds(ka, 1), :, :, :].reshape(grp, cw)
        r = jnp.dot(gb_ref[c], xg, preferred_element_type=F32)
        y2_ref[:, pl.ds(c, 1), pl.ds(ka, 1), :, :, :] = r.astype(BF16).reshape(NR, 1, 1, 2, NJ, cw)
        return carry

    lax.fori_loop(0, NR * NR, stage_b, 0, unroll=FFT_UNROLL)

    def stage_c(g, carry):
        kb, ka = g // NR, g % NR
        xg = y2_ref[pl.ds(kb, 1), :, pl.ds(ka, 1), :, :, :].reshape(grp, cw)
        r = jnp.dot(gc_ref[ka], xg, preferred_element_type=F32)
        y_ref[pl.ds(kb, 1), :, pl.ds(ka, 1), :, :, :] = r.astype(BF16).reshape(1, NR, 1, 2, NJ, cw)
        return carry

    lax.fori_loop(0, NR * NR, stage_c, 0, unroll=FFT_UNROLL)

    def stage_j(g, carry):
        m, kc = g // NR, g % NR
        xg = y_ref[pl.ds(2 * m, 2), pl.ds(kc, 1)].reshape(2 * grp, cw)
        r = jnp.dot(gj_ref[...], xg, preferred_element_type=F32).astype(BF16)
        z_ref[0, :, pl.ds(kc, 1), pl.ds(m, 1), :, 0:cw] = r[:grp].reshape(NJ, 1, 1, 2 * NR, cw)
        z_ref[0, :, pl.ds(kc, 1), pl.ds(m, 1), :, cw:] = r[grp:].reshape(NJ, 1, 1, 2 * NR, cw)
        return carry

    lax.fori_loop(0, NR * NR // 2, stage_j, 0, unroll=FFT_UNROLL // 2)


def _seq_dft(u):
    B, S, W = u.shape
    mats = [jnp.asarray(g).astype(BF16) for g in _dft_stage_matrices()]
    u6 = u.reshape(B, NR, NR, NR, NJ, W)
    const = lambda shp: pl.BlockSpec(shp, lambda b: (0,) * len(shp), pipeline_mode=pl.Buffered(1))
    z = pl.pallas_call(
        _fft_kernel,
        out_shape=jax.ShapeDtypeStruct((B, NJ, NR, NR // 2, 2 * NR, W), BF16),
        grid=(B,),
        in_specs=[pl.BlockSpec((1, NR, NR, NR, NJ, W), lambda b: (b, 0, 0, 0, 0, 0))] + [const(m.shape) for m in mats],
        out_specs=pl.BlockSpec((1, NJ, NR, NR // 2, 2 * NR, W), lambda b: (b, 0, 0, 0, 0, 0)),
        scratch_shapes=[pltpu.VMEM((NR, NR, NR, 2, NJ, FFT_CW), BF16)] * 2,
        compiler_params=pltpu.CompilerParams(dimension_semantics=("arbitrary",), vmem_limit_bytes=VMEM_LIMIT),
        name="seq_dft",
    )(u6, *mats)
    return z.reshape(B, S, W)


def _fold_kernel(cd_ref, sd_ref, wf_ref, m_ref, n_ref):
    m_ref[...] = jnp.zeros(m_ref.shape, m_ref.dtype)
    n_ref[...] = jnp.zeros(n_ref.shape, n_ref.dtype)
    for g in range(N_SLOTS):
        h, o = divmod(g, N_SLOTS // 2)
        w = wf_ref[g]
        blk = slice(o * HEAD_DIM, (o + 1) * HEAD_DIM)
        m_ref[h, blk, blk] = (0.5 * jnp.dot(cd_ref[...], w, preferred_element_type=F32,
                                            precision=lax.Precision.HIGHEST)).astype(BF16)
        n_ref[h, blk, blk] = (0.5 * jnp.dot(sd_ref[...], w, preferred_element_type=F32,
                                            precision=lax.Precision.HIGHEST)).astype(BF16)


def _fold_fourier_weights(w_fourier):
    cd, sd = _head_dim_dft()
    shp = jax.ShapeDtypeStruct((2, FFT_CW, FFT_CW), BF16)
    return pl.pallas_call(_fold_kernel, out_shape=[shp, shp], name="fold_fourier")(
        jnp.asarray(cd), jnp.asarray(sd), w_fourier.astype(F32))


FINAL_TM = 1024
PERM_ROWS = 256


def _final_kernel(x_ref, z_ref, zm_ref, zfirst_ref, gf_ref, ga_ref, o0_ref, o1_ref, o2_ref, l0_ref, l1_ref, l2_ref,
                  w1_ref, w2_ref, ex_ref, p1_ref, p2_ref, j1_ref, wo_ref, y_ref, l1_nat, l2_nat):
    tm = FINAL_TM
    n_sub = tm // PERM_ROWS
    n_sub_seq = zfirst_ref.shape[1]
    cw = FFT_CW
    first_row = lax.broadcasted_iota(jnp.int32, (PERM_ROWS, GROUP_W), 0) == 0
    slot_lane = lax.broadcasted_iota(jnp.int32, (tm, LANES), 1) < N_SLOTS

    for l_ref, l_nat, d in ((l1_ref, l1_nat, DILATIONS[1]), (l2_ref, l2_nat, DILATIONS[2])):
        for r in range(d):
            l_nat[pl.ds(r, tm // d, stride=d), :] = l_ref[0, r]

    zms = []
    for s in range(n_sub):
        oth = zm_ref[0, (n_sub - 1 - s) * PERM_ROWS:(n_sub - s) * PERM_ROWS, :]
        g = pl.program_id(1) * n_sub + s
        fix = zfirst_ref[0, pl.ds(lax.rem(n_sub_seq - g, n_sub_seq), 1), 0:1, :].reshape(1, GROUP_W).astype(F32)
        zms.append(jnp.where(first_row, fix, jnp.dot(j1_ref[...], oth, preferred_element_type=F32)))
    own = z_ref[0].astype(F32)
    zm = jnp.concatenate(zms, axis=0)
    zr, zi, mr, mi = own[:, :cw], own[:, cw:], zm[:, :cw], zm[:, cw:]
    halves = (((zr + mr), (zi - mi)), ((zi + mi), (mr - zr)))
    f = jnp.concatenate(
        [jnp.dot(re.astype(BF16), w1_ref[h], preferred_element_type=F32)
         + jnp.dot(im.astype(BF16), w2_ref[h], preferred_element_type=F32)
         for h, (re, im) in enumerate(halves)], axis=-1)
    gf = gf_ref[0].astype(F32)
    yf = (f * (gf * jax.nn.sigmoid(gf))).astype(BF16)

    o_groups = [o0_ref[0].astype(F32)]
    for o_ref, p_ref, d in ((o1_ref, p1_ref, DILATIONS[1]), (o2_ref, p2_ref, DILATIONS[2])):
        n = PERM_ROWS // d
        o_groups.append(jnp.concatenate(
            [jnp.dot(p_ref[...], jnp.concatenate([o_ref[0, r, s * n:(s + 1) * n, :] for r in range(d)], axis=0),
                     preferred_element_type=F32) for s in range(n_sub)], axis=0))
    sts = (l0_ref[0], l1_nat[...], l2_nat[...])
    mx = jnp.maximum(jnp.maximum(sts[0], sts[1]), sts[2])
    es = [jnp.exp2(st - mx) for st in sts]
    den = None
    for e_c, st in zip(es, sts):
        term = e_c * pltpu.roll(st, LANES - N_SLOTS, 1)
        den = term if den is None else den + term
    inv = 1.0 / jnp.where(slot_lane, den, 1.0)
    o = None
    for e_c, o_c in zip(es, o_groups):
        alpha = jnp.dot((e_c * inv).astype(BF16), ex_ref[...], preferred_element_type=F32)
        o = alpha * o_c if o is None else o + alpha * o_c
    ga = ga_ref[0].astype(F32)
    ya = (o * (ga * jax.nn.sigmoid(ga))).astype(BF16)

    mixed = jnp.dot(jnp.concatenate([yf, ya], axis=-1), wo_ref[...], preferred_element_type=F32)
    y_ref[0] = x_ref[0] + mixed


def _final(x, z, gf, ga, o_list, lse_list, w1, w2, w_out):
    B, S, D = x.shape
    tm = FINAL_TM
    nblk = S // tm
    n_sub_seq = S // PERM_ROWS
    d1, d2 = DILATIONS[1], DILATIONS[2]
    z_first = z.reshape(B, n_sub_seq, PERM_ROWS, GROUP_W)
    nat = lambda w: pl.BlockSpec((1, tm, w), lambda b, i: (b, i, 0))
    perm = lambda d, w: pl.BlockSpec((1, d, tm // d, w), lambda b, i: (b, 0, i, 0))
    const = lambda shp: pl.BlockSpec(shp, lambda b, i: (0,) * len(shp), pipeline_mode=pl.Buffered(1))
    p1 = jnp.asarray(_unpermute_matrix(PERM_ROWS, d1)).astype(BF16)
    p2 = jnp.asarray(_unpermute_matrix(PERM_ROWS, d2)).astype(BF16)
    j1 = jnp.asarray(_mirror_matrix(PERM_ROWS)).astype(BF16)
    ex = jnp.asarray(_slot_expand_matrix()).astype(BF16)
    return pl.pallas_call(
        _final_kernel,
        out_shape=jax.ShapeDtypeStruct((B, S, D), F32),
        grid=(B, nblk),
        in_specs=[nat(D), nat(GROUP_W), pl.BlockSpec((1, tm, GROUP_W), lambda b, i: (b, nblk - 1 - i, 0)),
                  pl.BlockSpec((1, n_sub_seq, 16, GROUP_W), lambda b, i: (b, 0, 0, 0)),
                  nat(GROUP_W), nat(GROUP_W),
                  nat(GROUP_W), perm(d1, GROUP_W), perm(d2, GROUP_W), nat(LANES), perm(d1, LANES), perm(d2, LANES),
                  const((2, FFT_CW, FFT_CW)), const((2, FFT_CW, FFT_CW)), const((LANES, GROUP_W)),
                  const((PERM_ROWS, PERM_ROWS)), const((PERM_ROWS, PERM_ROWS)), const((PERM_ROWS, PERM_ROWS)),
                  const((D, D))],
        out_specs=nat(D),
        scratch_shapes=[pltpu.VMEM((tm, LANES), F32), pltpu.VMEM((tm, LANES), F32)],
        compiler_params=pltpu.CompilerParams(
            dimension_semantics=("arbitrary", "arbitrary"), vmem_limit_bytes=VMEM_LIMIT),
        name="final_mix",
    )(x, z, z, z_first, gf, ga, *o_list, *lse_list, w1, w2, ex, p1, p2, j1, w_out)


def kernel(x, norm_w, w_in, q_norm_w, k_norm_w, w_fourier, w_out):
    B, S, D = x.shape
    n_cfg = len(DILATIONS)
    xf = x.astype(F32)
    gains = jnp.concatenate([q_norm_w.astype(F32).reshape(n_cfg, GROUP_W) * (HEAD_DIM ** -0.5 * LOG2E),
                             k_norm_w.astype(F32).reshape(n_cfg, GROUP_W)], axis=0)
    proj = _project(xf, norm_w.astype(F32), w_in.astype(BF16), gains)
    u_f, g_f, g_a = proj[:3]

    outs, lses = [], []
    for c, d in enumerate(DILATIONS):
        q, k, v = (t.reshape(B * d, S // d, GROUP_W) for t in proj[3 + 3 * c:6 + 3 * c])
        o, lse = _attention(q, k, v, d)
        if d == 1:
            outs.append(o)
            lses.append(lse)
        else:
            outs.append(o.reshape(B, d, S // d, GROUP_W))
            lses.append(lse.reshape(B, d, S // d, LANES))

    z = _seq_dft(u_f)
    w1, w2 = _fold_fourier_weights(w_fourier)
    y = _final(xf, z, g_f, g_a, outs, lses, w1, w2, w_out.astype(BF16))
    return y.astype(x.dtype)
```

```python
import functools

import numpy as np
import jax
import jax.numpy as jnp
from jax import lax
from jax.experimental import pallas as pl
from jax.experimental.pallas import tpu as pltpu

D_MODEL = 1024
HEAD_DIM = 64
N_SLOTS = 8
GROUP_W = N_SLOTS * HEAD_DIM
DILATIONS = (1, 4, 16)
RADIUS = 64
NORM_EPS = 1e-6
MASK_VALUE = -1e30
LOG2E = float(np.log2(np.e))
SEQ = 8192

NR, NJ = 8, 16

LANES = 128
V7X_VMEM_BYTES = 64 * 1024 * 1024
VMEM_LIMIT = V7X_VMEM_BYTES - 4 * 1024 * 1024

BF16 = jnp.bfloat16
F32 = jnp.float32


@functools.lru_cache(maxsize=None)
def _dft_stage_matrices():
    r = np.arange(NR, dtype=np.float64)
    j = np.arange(NJ, dtype=np.float64)
    eye_j = (j[:, None] == j[None, :]).astype(np.float64)[None, :, None, :]

    def radix_stage(phase):
        ph = 2 * np.pi * np.broadcast_to(phase, (NR, NJ, NR, NJ))
        c, s = np.cos(ph) * eye_j, np.sin(ph) * eye_j
        g = np.zeros((NR, 2, NJ, NR, 2, NJ))
        g[:, 0, :, :, 0, :] = c
        g[:, 0, :, :, 1, :] = s
        g[:, 1, :, :, 0, :] = -s
        g[:, 1, :, :, 1, :] = c
        return g.reshape(2 * NR * NJ, 2 * NR * NJ)

    kk = r[:, None, None, None]
    dig = r[None, None, :, None]
    jj = j[None, :, None, None]
    n_total = NJ * NR ** 3
    ga = np.stack([radix_stage(dig * kk / NR + bv * kk / NR ** 2 + jj * kk / n_total) for bv in range(NR)])
    gb = np.stack([radix_stage(dig * kk / NR + cv * kk / NR ** 2 + jj * kk / (NJ * NR ** 2)) for cv in range(NR)])
    gc = np.stack([radix_stage(dig * kk / NR + jj * kk / (NJ * NR) + dig * kav / NR ** 3) for kav in range(NR)])

    ph = 2 * np.pi * j[None, :] * j[:, None] / NJ
    c, s = np.cos(ph), np.sin(ph)
    gj = np.zeros((2, NJ, 2, NR, 2, NR, 2, NJ))
    for kb in range(2):
        for kav in range(NR):
            gj[0, :, kb, kav, kb, kav, 0, :] = c
            gj[0, :, kb, kav, kb, kav, 1, :] = s
            gj[1, :, kb, kav, kb, kav, 0, :] = -s
            gj[1, :, kb, kav, kb, kav, 1, :] = c
    gj = gj.reshape(4 * NR * NJ, 4 * NR * NJ)
    return tuple(np.asarray(g, dtype=np.float32) for g in (ga, gb, gc, gj))


@functools.lru_cache(maxsize=None)
def _mirror_matrix(rows):
    t = np.arange(rows)
    return np.asarray((t[:, None] + t[None, :] == rows) & (t[:, None] > 0), dtype=np.float32)


@functools.lru_cache(maxsize=None)
def _head_dim_dft():
    m = np.arange(HEAD_DIM, dtype=np.float64)
    ang = 2 * np.pi * m[:, None] * m[None, :] / HEAD_DIM
    scale = 1.0 / np.sqrt(SEQ * HEAD_DIM)
    return (np.asarray(np.cos(ang) * scale, dtype=np.float32), np.asarray(np.sin(ang) * scale, dtype=np.float32))


@functools.lru_cache(maxsize=None)
def _attn_bias(dilation):
    slopes = 2.0 ** (-8.0 * np.arange(1, N_SLOTS + 1, dtype=np.float64) / N_SLOTS)
    a = np.arange(2 * RADIUS)[:, None]
    c = np.arange(4 * RADIUS)[None, :]
    rel = np.abs(c - RADIUS - a)
    key_ok = (np.ones_like(c, dtype=bool), c >= RADIUS, c < 3 * RADIUS)
    tab = np.zeros((3, N_SLOTS, 2 * RADIUS, 4 * RADIUS))
    for var in range(3):
        for s in range(N_SLOTS):
            tab[var, s] = np.where((rel <= RADIUS) & key_ok[var], -slopes[s] * rel * dilation * LOG2E, MASK_VALUE)
    return np.asarray(tab.reshape(3, N_SLOTS // 2, 4 * RADIUS, 4 * RADIUS), dtype=np.float32)


@functools.lru_cache(maxsize=None)
def _slot_expand_matrix():
    r = np.arange(LANES)[:, None]
    c = np.arange(GROUP_W)[None, :]
    return np.asarray((r == c // HEAD_DIM), dtype=np.float32)


@functools.lru_cache(maxsize=None)
def _unpermute_matrix(tm, dilation):
    t = np.arange(tm)
    src = (t % dilation) * (tm // dilation) + t // dilation
    return np.asarray(src[:, None] == np.arange(tm)[None, :], dtype=np.float32)


PROJ_TM = 512
COL_UF, COL_GF, COL_Q, COL_K, COL_V, COL_GA = 0, 1, 2, 5, 8, 11


def _proj_kernel(x_ref, nw_ref, w_ref, gain_ref, uf_ref, gf_ref, ga_ref, *rest):
    qkv_refs, xw_ref, rs_ref = rest[:-2], rest[-2], rest[-1]
    x = x_ref[0]
    xw = x * nw_ref[...]
    ms = jnp.mean(x * x, axis=-1, keepdims=True) + NORM_EPS
    n_slab = xw_ref.shape[0]
    for s in range(n_slab):
        xw_ref[s] = xw[:, s * LANES:(s + 1) * LANES]
    rs_ref[0] = jnp.broadcast_to(lax.rsqrt(ms), (PROJ_TM, LANES))
    rs_ref[1] = jnp.broadcast_to(NORM_EPS * ms, (PROJ_TM, LANES))
    low = lax.broadcasted_iota(jnp.int32, (PROJ_TM, LANES), 1) < HEAD_DIM

    def project(lhs, row_scale, head_eps, col, gain_row):
        acc = jnp.dot(lhs, w_ref[:, col * GROUP_W:(col + 1) * GROUP_W], preferred_element_type=F32)
        tiles = []
        for t in range(GROUP_W // LANES):
            a = acc[:, t * LANES:(t + 1) * LANES]
            if gain_row is None:
                tiles.append(a * row_scale)
                continue
            z = a * a
            z_lo = jnp.where(low, z, 0.0)
            ms_lo = jnp.sum(z_lo, axis=-1, keepdims=True) * (1.0 / HEAD_DIM)
            ms_hi = jnp.sum(z - z_lo, axis=-1, keepdims=True) * (1.0 / HEAD_DIM)
            tiles.append(a * lax.rsqrt(jnp.where(low, ms_lo, ms_hi) + head_eps))
        out = jnp.concatenate(tiles, axis=-1)
        if gain_row is not None:
            out = out * gain_ref[gain_row:gain_row + 1, :]
        return out.astype(BF16)

    def regrouped(ref, idx, d):
        rows = PROJ_TM // d
        return jnp.concatenate([ref[idx, pl.ds(r, rows, stride=d), :] for r in range(d)], axis=0)

    lhs = xw.astype(BF16)
    row_scale, head_eps = rs_ref[0], rs_ref[1]
    uf_ref[0] = project(lhs, row_scale, head_eps, COL_UF, None)
    gf_ref[0] = project(lhs, row_scale, head_eps, COL_GF, None)
    ga_ref[0] = project(lhs, row_scale, head_eps, COL_GA, None)
    for c, d in enumerate(DILATIONS):
        q_ref, k_ref, v_ref = qkv_refs[3 * c:3 * c + 3]
        rows = PROJ_TM // d
        if d > 1:
            lhs = jnp.concatenate([regrouped(xw_ref, s, d) for s in range(n_slab)], axis=1).astype(BF16)
            row_scale, head_eps = regrouped(rs_ref, 0, d), regrouped(rs_ref, 1, d)
        for o_ref, col, gain_row in ((q_ref, COL_Q + c, c), (k_ref, COL_K + c, 3 + c), (v_ref, COL_V + c, None)):
            res = project(lhs, row_scale, head_eps, col, gain_row)
            for r in range(d):
                o_ref[0, r] = res[r * rows:(r + 1) * rows]


def _project(x, norm_w, w_in, gains):
    B, S, D = x.shape
    tm = PROJ_TM
    nat = jax.ShapeDtypeStruct((B, S, GROUP_W), BF16)
    nat_spec = pl.BlockSpec((1, tm, GROUP_W), lambda b, i: (b, i, 0))
    out_shape, out_specs = [nat] * 3, [nat_spec] * 3
    for d in DILATIONS:
        out_shape += [jax.ShapeDtypeStruct((B, d, S // d, GROUP_W), BF16)] * 3
        out_specs += [pl.BlockSpec((1, d, tm // d, GROUP_W), lambda b, i: (b, 0, i, 0))] * 3
    const = lambda shp: pl.BlockSpec(shp, lambda b, i: (0,) * len(shp), pipeline_mode=pl.Buffered(1))
    return pl.pallas_call(
        _proj_kernel,
        out_shape=out_shape,
        grid=(B, S // tm),
        in_specs=[pl.BlockSpec((1, tm, D), lambda b, i: (b, i, 0)),
                  const((1, D)), const(w_in.shape), const(gains.shape)],
        out_specs=out_specs,
        scratch_shapes=[pltpu.VMEM((D // LANES, tm, LANES), F32), pltpu.VMEM((2, tm, LANES), F32)],
        compiler_params=pltpu.CompilerParams(
            dimension_semantics=("arbitrary", "arbitrary"), vmem_limit_bytes=VMEM_LIMIT),
        name="proj",
    )(x, norm_w.reshape(1, D), w_in, gains)


ATT_STEP_ROWS = 4096
ATT_SUB = 2 * RADIUS


def _attn_kernel(q_ref, kp_ref, kc_ref, kn_ref, vp_ref, vc_ref, vn_ref, bias_ref, o_ref, st_ref, kbuf, vbuf):
    nb, ti = q_ref.shape[0], q_ref.shape[1]
    i = pl.program_id(1)
    first = i == 0
    last = i == pl.num_programs(1) - 1
    lane = lax.broadcasted_iota(jnp.int32, (ATT_SUB, LANES), 1)
    low = lane < HEAD_DIM
    ones = jnp.ones((2 * ATT_SUB, LANES), BF16)
    n_sub = ti // ATT_SUB
    for b in range(nb):
        kbuf[b, 0:RADIUS] = kp_ref[b]
        kbuf[b, RADIUS:RADIUS + ti] = kc_ref[b]
        kbuf[b, RADIUS + ti:] = kn_ref[b]
        vbuf[b, 0:RADIUS] = vp_ref[b]
        vbuf[b, RADIUS:RADIUS + ti] = vc_ref[b]
        vbuf[b, RADIUS + ti:] = vn_ref[b]
        for j in range(n_sub):
            r0 = j * ATT_SUB
            variant = 0
            if j == 0:
                variant = jnp.where(first, 1, 0)
            if j == n_sub - 1:
                variant = jnp.where(last, 2, 0)
            st = jnp.zeros((ATT_SUB, LANES), F32)
            for p in range(N_SLOTS // 2):
                c0 = p * LANES
                qp = q_ref[b, r0:r0 + ATT_SUB, c0:c0 + LANES]
                kw = kbuf[b, r0:r0 + 2 * ATT_SUB, c0:c0 + LANES]
                vw = jnp.concatenate([vbuf[b, r0:r0 + 2 * ATT_SUB, c0:c0 + LANES], ones], axis=1)
                zero = jnp.zeros_like(qp)
                qs = jnp.concatenate([jnp.where(low, qp, zero), jnp.where(low, zero, qp)], axis=0)
                s = lax.dot_general(qs, kw, (((1,), (1,)), ((), ())), preferred_element_type=F32)
                s = s + bias_ref[variant, p]
                m = jnp.max(s, axis=-1, keepdims=True)
                e = jnp.exp2(s - m).astype(BF16)
                ol = jnp.dot(e, vw, preferred_element_type=F32)
                o_ref[b, r0:r0 + ATT_SUB, c0:c0 + LANES] = jnp.where(
                    low, ol[:ATT_SUB, :LANES], ol[ATT_SUB:, :LANES]).astype(BF16)
                st = jnp.where(lane == 2 * p, m[:ATT_SUB], st)
                st = jnp.where(lane == 2 * p + 1, m[ATT_SUB:], st)
                st = jnp.where(lane == N_SLOTS + 2 * p, ol[:ATT_SUB, LANES:], st)
                st = jnp.where(lane == N_SLOTS + 2 * p + 1, ol[ATT_SUB:, LANES:], st)
            st_ref[b, r0:r0 + ATT_SUB, :] = st


def _attention(q, k, v, dilation):
    N, L, W = q.shape
    ti = min(L, ATT_STEP_ROWS)
    nb = ATT_STEP_ROWS // ti
    nblk = L // ti
    halo_per_blk = ti // RADIUS
    n_halo = L // RADIUS
    bias = jnp.asarray(_attn_bias(dilation))
    cur = pl.BlockSpec((nb, ti, W), lambda n, i: (n, i, 0))
    prev = pl.BlockSpec((nb, RADIUS, W), lambda n, i: (n, jnp.maximum(i * halo_per_blk - 1, 0), 0))
    nxt = pl.BlockSpec((nb, RADIUS, W), lambda n, i: (n, jnp.minimum((i + 1) * halo_per_blk, n_halo - 1), 0))
    return pl.pallas_call(
        _attn_kernel,
        out_shape=[jax.ShapeDtypeStruct((N, L, W), BF16), jax.ShapeDtypeStruct((N, L, LANES), F32)],
        grid=(N // nb, nblk),
        in_specs=[cur, prev, cur, nxt, prev, cur, nxt,
                  pl.BlockSpec(bias.shape, lambda n, i: (0, 0, 0, 0), pipeline_mode=pl.Buffered(1))],
        out_specs=[pl.BlockSpec((nb, ti, W), lambda n, i: (n, i, 0)),
                   pl.BlockSpec((nb, ti, LANES), lambda n, i: (n, i, 0))],
        scratch_shapes=[pltpu.VMEM((nb, ti + 2 * RADIUS, W), BF16), pltpu.VMEM((nb, ti + 2 * RADIUS, W), BF16)],
        compiler_params=pltpu.CompilerParams(
            dimension_semantics=("arbitrary", "arbitrary"), vmem_limit_bytes=VMEM_LIMIT),
        name=f"attn_d{dilation}",
    )(q, k, k, k, v, v, v, bias)


FFT_CW = GROUP_W // 2
FFT_UNROLL = NR * NR


def _fft_kernel(u_ref, ga_ref, gb_ref, gc_ref, gj_ref, z_ref, y_ref, y2_ref):
    cw = FFT_CW
    grp = NR * 2 * NJ

    def stage_a(g, carry):
        b, c = g // NR, g % NR
        u = u_ref[0, :, pl.ds(b, 1), pl.ds(c, 1), :, :].reshape(NR, NJ, 2 * cw)
        xg = jnp.concatenate([u[:, :, :cw], u[:, :, cw:]], axis=1).reshape(grp, cw)
        r = jnp.dot(ga_ref[b], xg, preferred_element_type=F32)
        y_ref[pl.ds(b, 1), pl.ds(c, 1)] = r.astype(BF16).reshape(1, 1, NR, 2, NJ, cw)
        return carry

    lax.fori_loop(0, NR * NR, stage_a, 0, unroll=FFT_UNROLL)

    def stage_b(g, carry):
        c, ka = g // NR, g % NR
        xg = y_ref[:, pl.ds(c, 1), pl.ds(ka, 1), :, :, :].reshape(grp, cw)
        r = jnp.dot(gb_ref[c], xg, preferred_element_type=F32)
        y2_ref[:, pl.ds(c, 1), pl.ds(ka, 1), :, :, :] = r.astype(BF16).reshape(NR, 1, 1, 2, NJ, cw)
        return carry

    lax.fori_loop(0, NR * NR, stage_b, 0, unroll=FFT_UNROLL)

    def stage_c(g, carry):
        kb, ka = g // NR, g % NR
        xg = y2_ref[pl.ds(kb, 1), :, pl.ds(ka, 1), :, :, :].reshape(grp, cw)
        r = jnp.dot(gc_ref[ka], xg, preferred_element_type=F32)
        y_ref[pl.ds(kb, 1), :, pl.ds(ka, 1), :, :, :] = r.astype(BF16).reshape(1, NR, 1, 2, NJ, cw)
        return carry

    lax.fori_loop(0, NR * NR, stage_c, 0, unroll=FFT_UNROLL)

    def stage_j(g, carry):
        m, kc = g // NR, g % NR
        xg = y_ref[pl.ds(2 * m, 2), pl.ds(kc, 1)].reshape(2 * grp, cw)
        r = jnp.dot(gj_ref[...], xg, preferred_element_type=F32).astype(BF16)
        z_ref[0, :, pl.ds(kc, 1), pl.ds(m, 1), :, 0:cw] = r[:grp].reshape(NJ, 1, 1, 2 * NR, cw)
        z_ref[0, :, pl.ds(kc, 1), pl.ds(m, 1), :, cw:] = r[grp:].reshape(NJ, 1, 1, 2 * NR, cw)
        return carry

    lax.fori_loop(0, NR * NR // 2, stage_j, 0, unroll=FFT_UNROLL // 2)


def _seq_dft(u):
    B, S, W = u.shape
    mats = [jnp.asarray(g).astype(BF16) for g in _dft_stage_matrices()]
    u6 = u.reshape(B, NR, NR, NR, NJ, W)
    const = lambda shp: pl.BlockSpec(shp, lambda b: (0,) * len(shp), pipeline_mode=pl.Buffered(1))
    z = pl.pallas_call(
        _fft_kernel,
        out_shape=jax.ShapeDtypeStruct((B, NJ, NR, NR // 2, 2 * NR, W), BF16),
        grid=(B,),
        in_specs=[pl.BlockSpec((1, NR, NR, NR, NJ, W), lambda b: (b, 0, 0, 0, 0, 0))] + [const(m.shape) for m in mats],
        out_specs=pl.BlockSpec((1, NJ, NR, NR // 2, 2 * NR, W), lambda b: (b, 0, 0, 0, 0, 0)),
        scratch_shapes=[pltpu.VMEM((NR, NR, NR, 2, NJ, FFT_CW), BF16)] * 2,
        compiler_params=pltpu.CompilerParams(dimension_semantics=("arbitrary",), vmem_limit_bytes=VMEM_LIMIT),
        name="seq_dft",
    )(u6, *mats)
    return z.reshape(B, S, W)


def _fold_kernel(cd_ref, sd_ref, wf_ref, m_ref, n_ref):
    m_ref[...] = jnp.zeros(m_ref.shape, m_ref.dtype)
    n_ref[...] = jnp.zeros(n_ref.shape, n_ref.dtype)
    for g in range(N_SLOTS):
        h, o = divmod(g, N_SLOTS // 2)
        w = wf_ref[g]
        blk = slice(o * HEAD_DIM, (o + 1) * HEAD_DIM)
        m_ref[h, blk, blk] = (0.5 * jnp.dot(cd_ref[...], w, preferred_element_type=F32,
                                            precision=lax.Precision.HIGHEST)).astype(BF16)
        n_ref[h, blk, blk] = (0.5 * jnp.dot(sd_ref[...], w, preferred_element_type=F32,
                                            precision=lax.Precision.HIGHEST)).astype(BF16)


def _fold_fourier_weights(w_fourier):
    cd, sd = _head_dim_dft()
    shp = jax.ShapeDtypeStruct((2, FFT_CW, FFT_CW), BF16)
    return pl.pallas_call(_fold_kernel, out_shape=[shp, shp], name="fold_fourier")(
        jnp.asarray(cd), jnp.asarray(sd), w_fourier.astype(F32))


FINAL_TM = 1024
PERM_ROWS = 256


X_RING = 3


def _final_kernel(x_hbm, z_ref, zm_ref, zfirst_ref, gf_ref, ga_ref, o0_ref, o1_ref, o2_ref, l0_ref, l1_ref, l2_ref,
                  w1_ref, w2_ref, ex_ref, p1_ref, p2_ref, j1_ref, wo_ref, y_ref, l1_nat, l2_nat, x_buf, x_sem):
    tm = FINAL_TM
    nblk = pl.num_programs(1)
    step = pl.program_id(0) * nblk + pl.program_id(1)
    n_steps = pl.num_programs(0) * nblk

    def x_copy(s):
        slot = lax.rem(s, X_RING)
        return pltpu.make_async_copy(
            x_hbm.at[lax.div(s, nblk), pl.ds(lax.rem(s, nblk) * tm, tm), :], x_buf.at[slot], x_sem.at[slot])

    @pl.when(step == 0)
    def _():
        for s in range(X_RING - 1):
            x_copy(jnp.int32(s)).start()

    @pl.when(step + (X_RING - 1) < n_steps)
    def _():
        x_copy(step + (X_RING - 1)).start()

    n_sub = tm // PERM_ROWS
    n_sub_seq = zfirst_ref.shape[1]
    cw = FFT_CW
    first_row = lax.broadcasted_iota(jnp.int32, (PERM_ROWS, GROUP_W), 0) == 0
    slot_lane = lax.broadcasted_iota(jnp.int32, (tm, LANES), 1) < N_SLOTS

    for l_ref, l_nat, d in ((l1_ref, l1_nat, DILATIONS[1]), (l2_ref, l2_nat, DILATIONS[2])):
        for r in range(d):
            l_nat[pl.ds(r, tm // d, stride=d), :] = l_ref[0, r]

    zms = []
    for s in range(n_sub):
        oth = zm_ref[0, (n_sub - 1 - s) * PERM_ROWS:(n_sub - s) * PERM_ROWS, :]
        g = pl.program_id(1) * n_sub + s
        fix = zfirst_ref[0, pl.ds(lax.rem(n_sub_seq - g, n_sub_seq), 1), 0:1, :].reshape(1, GROUP_W).astype(F32)
        zms.append(jnp.where(first_row, fix, jnp.dot(j1_ref[...], oth, preferred_element_type=F32)))
    own = z_ref[0].astype(F32)
    zm = jnp.concatenate(zms, axis=0)
    zr, zi, mr, mi = own[:, :cw], own[:, cw:], zm[:, :cw], zm[:, cw:]
    halves = (((zr + mr), (zi - mi)), ((zi + mi), (mr - zr)))
    f = jnp.concatenate(
        [jnp.dot(re.astype(BF16), w1_ref[h], preferred_element_type=F32)
         + jnp.dot(im.astype(BF16), w2_ref[h], preferred_element_type=F32)
         for h, (re, im) in enumerate(halves)], axis=-1)
    gf = gf_ref[0].astype(F32)
    yf = (f * (gf * jax.nn.sigmoid(gf))).astype(BF16)

    o_groups = [o0_ref[0].astype(F32)]
    for o_ref, p_ref, d in ((o1_ref, p1_ref, DILATIONS[1]), (o2_ref, p2_ref, DILATIONS[2])):
        n = PERM_ROWS // d
        o_groups.append(jnp.concatenate(
            [jnp.dot(p_ref[...], jnp.concatenate([o_ref[0, r, s * n:(s + 1) * n, :] for r in range(d)], axis=0),
                     preferred_element_type=F32) for s in range(n_sub)], axis=0))
    sts = (l0_ref[0], l1_nat[...], l2_nat[...])
    mx = jnp.maximum(jnp.maximum(sts[0], sts[1]), sts[2])
    es = [jnp.exp2(st - mx) for st in sts]
    den = None
    for e_c, st in zip(es, sts):
        term = e_c * pltpu.roll(st, LANES - N_SLOTS, 1)
        den = term if den is None else den + term
    inv = 1.0 / jnp.where(slot_lane, den, 1.0)
    o = None
    for e_c, o_c in zip(es, o_groups):
        alpha = jnp.dot((e_c * inv).astype(BF16), ex_ref[...], preferred_element_type=F32)
        o = alpha * o_c if o is None else o + alpha * o_c
    ga = ga_ref[0].astype(F32)
    ya = (o * (ga * jax.nn.sigmoid(ga))).astype(BF16)

    mixed = jnp.dot(jnp.concatenate([yf, ya], axis=-1), wo_ref[...], preferred_element_type=F32)
    x_copy(step).wait()
    y_ref[0] = x_buf[lax.rem(step, X_RING)] + mixed


def _final(x, z, gf, ga, o_list, lse_list, w1, w2, w_out):
    B, S, D = x.shape
    tm = FINAL_TM
    nblk = S // tm
    n_sub_seq = S // PERM_ROWS
    d1, d2 = DILATIONS[1], DILATIONS[2]
    z_first = z.reshape(B, n_sub_seq, PERM_ROWS, GROUP_W)
    nat = lambda w: pl.BlockSpec((1, tm, w), lambda b, i: (b, i, 0))
    perm = lambda d, w: pl.BlockSpec((1, d, tm // d, w), lambda b, i: (b, 0, i, 0))
    const = lambda shp: pl.BlockSpec(shp, lambda b, i: (0,) * len(shp), pipeline_mode=pl.Buffered(1))
    p1 = jnp.asarray(_unpermute_matrix(PERM_ROWS, d1)).astype(BF16)
    p2 = jnp.asarray(_unpermute_matrix(PERM_ROWS, d2)).astype(BF16)
    j1 = jnp.asarray(_mirror_matrix(PERM_ROWS)).astype(BF16)
    ex = jnp.asarray(_slot_expand_matrix()).astype(BF16)
    return pl.pallas_call(
        _final_kernel,
        out_shape=jax.ShapeDtypeStruct((B, S, D), F32),
        grid=(B, nblk),
        in_specs=[pl.BlockSpec(memory_space=pl.ANY),
                  nat(GROUP_W), pl.BlockSpec((1, tm, GROUP_W), lambda b, i: (b, nblk - 1 - i, 0)),
                  pl.BlockSpec((1, n_sub_seq, 16, GROUP_W), lambda b, i: (b, 0, 0, 0)),
                  nat(GROUP_W), nat(GROUP_W),
                  nat(GROUP_W), perm(d1, GROUP_W), perm(d2, GROUP_W), nat(LANES), perm(d1, LANES), perm(d2, LANES),
                  const((2, FFT_CW, FFT_CW)), const((2, FFT_CW, FFT_CW)), const((LANES, GROUP_W)),
                  const((PERM_ROWS, PERM_ROWS)), const((PERM_ROWS, PERM_ROWS)), const((PERM_ROWS, PERM_ROWS)),
                  const((D, D))],
        out_specs=nat(D),
        scratch_shapes=[pltpu.VMEM((tm, LANES), F32), pltpu.VMEM((tm, LANES), F32),
                        pltpu.VMEM((X_RING, tm, D), F32), pltpu.SemaphoreType.DMA((X_RING,))],
        compiler_params=pltpu.CompilerParams(
            dimension_semantics=("arbitrary", "arbitrary"), vmem_limit_bytes=VMEM_LIMIT),
        name="final_mix",
    )(x, z, z, z_first, gf, ga, *o_list, *lse_list, w1, w2, ex, p1, p2, j1, w_out)


def kernel(x, norm_w, w_in, q_norm_w, k_norm_w, w_fourier, w_out):
    B, S, D = x.shape
    n_cfg = len(DILATIONS)
    xf = x.astype(F32)
    gains = jnp.concatenate([q_norm_w.astype(F32).reshape(n_cfg, GROUP_W) * (HEAD_DIM ** -0.5 * LOG2E),
                             k_norm_w.astype(F32).reshape(n_cfg, GROUP_W)], axis=0)
    proj = _project(xf, norm_w.astype(F32), w_in.astype(BF16), gains)
    u_f, g_f, g_a = proj[:3]

    outs, lses = [], []
    for c, d in enumerate(DILATIONS):
        q, k, v = (t.reshape(B * d, S // d, GROUP_W) for t in proj[3 + 3 * c:6 + 3 * c])
        o, lse = _attention(q, k, v, d)
        if d == 1:
            outs.append(o)
            lses.append(lse)
        else:
            outs.append(o.reshape(B, d, S // d, GROUP_W))
            lses.append(lse.reshape(B, d, S // d, LANES))

    z = _seq_dft(u_f)
    w1, w2 = _fold_fourier_weights(w_fourier)
    y = _final(xf, z, g_f, g_a, outs, lses, w1, w2, w_out.astype(BF16))
    return y.astype(x.dtype)
```

```python
import functools

import numpy as np
import jax
import jax.numpy as jnp
from jax import lax
from jax.experimental import pallas as pl
from jax.experimental.pallas import tpu as pltpu

D_MODEL = 1024
HEAD_DIM = 64
N_SLOTS = 8
GROUP_W = N_SLOTS * HEAD_DIM
DILATIONS = (1, 4, 16)
RADIUS = 64
NORM_EPS = 1e-6
MASK_VALUE = -1e30
LOG2E = float(np.log2(np.e))
SEQ = 8192

NR, NJ = 8, 16

LANES = 128
V7X_VMEM_BYTES = 64 * 1024 * 1024
VMEM_LIMIT = V7X_VMEM_BYTES - 4 * 1024 * 1024

BF16 = jnp.bfloat16
F32 = jnp.float32


@functools.lru_cache(maxsize=None)
def _dft_stage_matrices():
    r = np.arange(NR, dtype=np.float64)
    j = np.arange(NJ, dtype=np.float64)
    eye_j = (j[:, None] == j[None, :]).astype(np.float64)[None, :, None, :]

    def radix_stage(phase):
        ph = 2 * np.pi * np.broadcast_to(phase, (NR, NJ, NR, NJ))
        c, s = np.cos(ph) * eye_j, np.sin(ph) * eye_j
        g = np.zeros((NR, 2, NJ, NR, 2, NJ))
        g[:, 0, :, :, 0, :] = c
        g[:, 0, :, :, 1, :] = s
        g[:, 1, :, :, 0, :] = -s
        g[:, 1, :, :, 1, :] = c
        return g.reshape(2 * NR * NJ, 2 * NR * NJ)

    kk = r[:, None, None, None]
    dig = r[None, None, :, None]
    jj = j[None, :, None, None]
    n_total = NJ * NR ** 3
    ga = np.stack([radix_stage(dig * kk / NR + bv * kk / NR ** 2 + jj * kk / n_total) for bv in range(NR)])
    gb = np.stack([radix_stage(dig * kk / NR + cv * kk / NR ** 2 + jj * kk / (NJ * NR ** 2)) for cv in range(NR)])
    gc = np.stack([radix_stage(dig * kk / NR + jj * kk / (NJ * NR) + dig * kav / NR ** 3) for kav in range(NR)])

    ph = 2 * np.pi * j[None, :] * j[:, None] / NJ
    c, s = np.cos(ph), np.sin(ph)
    gj = np.zeros((2, NJ, 2, NR, 2, NR, 2, NJ))
    for kb in range(2):
        for kav in range(NR):
            gj[0, :, kb, kav, kb, kav, 0, :] = c
            gj[0, :, kb, kav, kb, kav, 1, :] = s
            gj[1, :, kb, kav, kb, kav, 0, :] = -s
            gj[1, :, kb, kav, kb, kav, 1, :] = c
    gj = gj.reshape(4 * NR * NJ, 4 * NR * NJ)
    return tuple(np.asarray(g, dtype=np.float32) for g in (ga, gb, gc, gj))


@functools.lru_cache(maxsize=None)
def _mirror_matrix(rows):
    t = np.arange(rows)
    return np.asarray((t[:, None] + t[None, :] == rows) & (t[:, None] > 0), dtype=np.float32)


@functools.lru_cache(maxsize=None)
def _head_dim_dft():
    m = np.arange(HEAD_DIM, dtype=np.float64)
    ang = 2 * np.pi * m[:, None] * m[None, :] / HEAD_DIM
    scale = 1.0 / np.sqrt(SEQ * HEAD_DIM)
    return (np.asarray(np.cos(ang) * scale, dtype=np.float32), np.asarray(np.sin(ang) * scale, dtype=np.float32))


@functools.lru_cache(maxsize=None)
def _attn_bias(dilation):
    slopes = 2.0 ** (-8.0 * np.arange(1, N_SLOTS + 1, dtype=np.float64) / N_SLOTS)
    a = np.arange(2 * RADIUS)[:, None]
    c = np.arange(4 * RADIUS)[None, :]
    rel = np.abs(c - RADIUS - a)
    key_ok = (np.ones_like(c, dtype=bool), c >= RADIUS, c < 3 * RADIUS)
    tab = np.zeros((3, N_SLOTS, 2 * RADIUS, 4 * RADIUS))
    for var in range(3):
        for s in range(N_SLOTS):
            tab[var, s] = np.where((rel <= RADIUS) & key_ok[var], -slopes[s] * rel * dilation * LOG2E, MASK_VALUE)
    return np.asarray(tab.reshape(3, N_SLOTS // 2, 4 * RADIUS, 4 * RADIUS), dtype=np.float32)


@functools.lru_cache(maxsize=None)
def _slot_expand_matrix():
    r = np.arange(LANES)[:, None]
    c = np.arange(GROUP_W)[None, :]
    return np.asarray((r == c // HEAD_DIM), dtype=np.float32)


@functools.lru_cache(maxsize=None)
def _unpermute_matrix(tm, dilation):
    t = np.arange(tm)
    src = (t % dilation) * (tm // dilation) + t // dilation
    return np.asarray(src[:, None] == np.arange(tm)[None, :], dtype=np.float32)


PROJ_TM = 512
COL_UF, COL_GF, COL_Q, COL_K, COL_V, COL_GA = 0, 1, 2, 5, 8, 11


def _proj_kernel(x_ref, nw_ref, w_ref, gain_ref, uf_ref, gf_ref, ga_ref, *rest):
    qkv_refs, (xw_ref, rs_ref, lvl_x_ref, lvl_rs_ref) = rest[:-4], rest[-4:]
    x = x_ref[0]
    xw = x * nw_ref[...]
    ms = jnp.mean(x * x, axis=-1, keepdims=True) + NORM_EPS
    n_slab = xw_ref.shape[0]
    for s in range(n_slab):
        xw_ref[s] = xw[:, s * LANES:(s + 1) * LANES]
    rs_ref[0] = jnp.broadcast_to(lax.rsqrt(ms), (PROJ_TM, LANES))
    rs_ref[1] = jnp.broadcast_to(NORM_EPS * ms, (PROJ_TM, LANES))
    low = lax.broadcasted_iota(jnp.int32, (PROJ_TM, LANES), 1) < HEAD_DIM

    def project(lhs, row_scale, head_eps, col, gain_row):
        acc = jnp.dot(lhs, w_ref[:, col * GROUP_W:(col + 1) * GROUP_W], preferred_element_type=F32)
        tiles = []
        for t in range(GROUP_W // LANES):
            a = acc[:, t * LANES:(t + 1) * LANES]
            if gain_row is None:
                tiles.append(a * row_scale)
                continue
            z = a * a
            z_lo = jnp.where(low, z, 0.0)
            ms_lo = jnp.sum(z_lo, axis=-1, keepdims=True) * (1.0 / HEAD_DIM)
            ms_hi = jnp.sum(z - z_lo, axis=-1, keepdims=True) * (1.0 / HEAD_DIM)
            tiles.append(a * lax.rsqrt(jnp.where(low, ms_lo, ms_hi) + head_eps))
        out = jnp.concatenate(tiles, axis=-1)
        if gain_row is not None:
            out = out * gain_ref[gain_row:gain_row + 1, :]
        return out.astype(BF16)

    lhs = xw.astype(BF16)
    row_scale, head_eps = rs_ref[0], rs_ref[1]
    uf_ref[0] = project(lhs, row_scale, head_eps, COL_UF, None)
    gf_ref[0] = project(lhs, row_scale, head_eps, COL_GF, None)
    ga_ref[0] = project(lhs, row_scale, head_eps, COL_GA, None)
    residues, src_x, src_rs, d_prev = [0], xw_ref, rs_ref, 1
    for c, d in enumerate(DILATIONS):
        q_ref, k_ref, v_ref = qkv_refs[3 * c:3 * c + 3]
        rows = PROJ_TM // d
        if d > 1:
            step, rows_prev, n_prev = d // d_prev, PROJ_TM // d_prev, len(residues)

            def split(ref, idx):
                return jnp.concatenate([ref[idx, pl.ds(p * rows_prev + r, rows, stride=step), :]
                                        for p in range(n_prev) for r in range(step)], axis=0)

            x32 = [split(src_x, s) for s in range(n_slab)]
            row_scale, head_eps = split(src_rs, 0), split(src_rs, 1)
            residues = [r * d_prev + res for res in residues for r in range(step)]
            if d != DILATIONS[-1]:
                for s in range(n_slab):
                    lvl_x_ref[s] = x32[s]
                lvl_rs_ref[0], lvl_rs_ref[1] = row_scale, head_eps
                src_x, src_rs = lvl_x_ref, lvl_rs_ref
            lhs = jnp.concatenate(x32, axis=1).astype(BF16)
            d_prev = d
        for o_ref, col, gain_row in ((q_ref, COL_Q + c, c), (k_ref, COL_K + c, 3 + c), (v_ref, COL_V + c, None)):
            res = project(lhs, row_scale, head_eps, col, gain_row)
            for q, resid in enumerate(residues):
                o_ref[0, resid] = res[q * rows:(q + 1) * rows]


def _project(x, norm_w, w_in, gains):
    B, S, D = x.shape
    tm = PROJ_TM
    nat = jax.ShapeDtypeStruct((B, S, GROUP_W), BF16)
    nat_spec = pl.BlockSpec((1, tm, GROUP_W), lambda b, i: (b, i, 0))
    out_shape, out_specs = [nat] * 3, [nat_spec] * 3
    for d in DILATIONS:
        out_shape += [jax.ShapeDtypeStruct((B, d, S // d, GROUP_W), BF16)] * 3
        out_specs += [pl.BlockSpec((1, d, tm // d, GROUP_W), lambda b, i: (b, 0, i, 0))] * 3
    const = lambda shp: pl.BlockSpec(shp, lambda b, i: (0,) * len(shp), pipeline_mode=pl.Buffered(1))
    return pl.pallas_call(
        _proj_kernel,
        out_shape=out_shape,
        grid=(B, S // tm),
        in_specs=[pl.BlockSpec((1, tm, D), lambda b, i: (b, i, 0)),
                  const((1, D)), const(w_in.shape), const(gains.shape)],
        out_specs=out_specs,
        scratch_shapes=[pltpu.VMEM((D // LANES, tm, LANES), F32), pltpu.VMEM((2, tm, LANES), F32)] * 2,
        compiler_params=pltpu.CompilerParams(
            dimension_semantics=("arbitrary", "arbitrary"), vmem_limit_bytes=VMEM_LIMIT),
        name="proj",
    )(x, norm_w.reshape(1, D), w_in, gains)


ATT_STEP_ROWS = 4096
ATT_SUB = 2 * RADIUS


def _attn_kernel(q_ref, kp_ref, kc_ref, kn_ref, vp_ref, vc_ref, vn_ref, bias_ref, o_ref, st_ref, kbuf, vbuf):
    nb, ti = q_ref.shape[0], q_ref.shape[1]
    i = pl.program_id(1)
    first = i == 0
    last = i == pl.num_programs(1) - 1
    lane = lax.broadcasted_iota(jnp.int32, (ATT_SUB, LANES), 1)
    low = lane < HEAD_DIM
    ones = jnp.ones((2 * ATT_SUB, LANES), BF16)
    n_sub = ti // ATT_SUB
    for b in range(nb):
        kbuf[b, 0:RADIUS] = kp_ref[b]
        kbuf[b, RADIUS:RADIUS + ti] = kc_ref[b]
        kbuf[b, RADIUS + ti:] = kn_ref[b]
        vbuf[b, 0:RADIUS] = vp_ref[b]
        vbuf[b, RADIUS:RADIUS + ti] = vc_ref[b]
        vbuf[b, RADIUS + ti:] = vn_ref[b]
        for j in range(n_sub):
            r0 = j * ATT_SUB
            variant = 0
            if j == 0:
                variant = jnp.where(first, 1, 0)
            if j == n_sub - 1:
                variant = jnp.where(last, 2, 0)
            st = jnp.zeros((ATT_SUB, LANES), F32)
            for p in range(N_SLOTS // 2):
                c0 = p * LANES
                qp = q_ref[b, r0:r0 + ATT_SUB, c0:c0 + LANES]
                kw = kbuf[b, r0:r0 + 2 * ATT_SUB, c0:c0 + LANES]
                vw = jnp.concatenate([vbuf[b, r0:r0 + 2 * ATT_SUB, c0:c0 + LANES], ones], axis=1)
                zero = jnp.zeros_like(qp)
                qs = jnp.concatenate([jnp.where(low, qp, zero), jnp.where(low, zero, qp)], axis=0)
                s = lax.dot_general(qs, kw, (((1,), (1,)), ((), ())), preferred_element_type=F32)
                s = s + bias_ref[variant, p]
                m = jnp.max(s, axis=-1, keepdims=True)
                e = jnp.exp2(s - m).astype(BF16)
                ol = jnp.dot(e, vw, preferred_element_type=F32)
                o_ref[b, r0:r0 + ATT_SUB, c0:c0 + LANES] = jnp.where(
                    low, ol[:ATT_SUB, :LANES], ol[ATT_SUB:, :LANES]).astype(BF16)
                st = jnp.where(lane == 2 * p, m[:ATT_SUB], st)
                st = jnp.where(lane == 2 * p + 1, m[ATT_SUB:], st)
                st = jnp.where(lane == N_SLOTS + 2 * p, ol[:ATT_SUB, LANES:], st)
                st = jnp.where(lane == N_SLOTS + 2 * p + 1, ol[ATT_SUB:, LANES:], st)
            st_ref[b, r0:r0 + ATT_SUB, :] = st


def _attention(q, k, v, dilation):
    N, L, W = q.shape
    ti = min(L, ATT_STEP_ROWS)
    nb = ATT_STEP_ROWS // ti
    nblk = L // ti
    halo_per_blk = ti // RADIUS
    n_halo = L // RADIUS
    bias = jnp.asarray(_attn_bias(dilation))
    cur = pl.BlockSpec((nb, ti, W), lambda n, i: (n, i, 0))
    prev = pl.BlockSpec((nb, RADIUS, W), lambda n, i: (n, jnp.maximum(i * halo_per_blk - 1, 0), 0))
    nxt = pl.BlockSpec((nb, RADIUS, W), lambda n, i: (n, jnp.minimum((i + 1) * halo_per_blk, n_halo - 1), 0))
    return pl.pallas_call(
        _attn_kernel,
        out_shape=[jax.ShapeDtypeStruct((N, L, W), BF16), jax.ShapeDtypeStruct((N, L, LANES), F32)],
        grid=(N // nb, nblk),
        in_specs=[cur, prev, cur, nxt, prev, cur, nxt,
                  pl.BlockSpec(bias.shape, lambda n, i: (0, 0, 0, 0), pipeline_mode=pl.Buffered(1))],
        out_specs=[pl.BlockSpec((nb, ti, W), lambda n, i: (n, i, 0)),
                   pl.BlockSpec((nb, ti, LANES), lambda n, i: (n, i, 0))],
        scratch_shapes=[pltpu.VMEM((nb, ti + 2 * RADIUS, W), BF16), pltpu.VMEM((nb, ti + 2 * RADIUS, W), BF16)],
        compiler_params=pltpu.CompilerParams(
            dimension_semantics=("arbitrary", "arbitrary"), vmem_limit_bytes=VMEM_LIMIT),
        name=f"attn_d{dilation}",
    )(q, k, k, k, v, v, v, bias)


FFT_CW = GROUP_W // 2
FFT_UNROLL = NR * NR


def _fft_kernel(u_ref, ga_ref, gb_ref, gc_ref, gj_ref, z_ref, y_ref, y2_ref):
    cw = FFT_CW
    grp = NR * 2 * NJ

    def stage_a(g, carry):
        b, c = g // NR, g % NR
        u = u_ref[0, :, pl.ds(b, 1), pl.ds(c, 1), :, :].reshape(NR, NJ, 2 * cw)
        xg = jnp.concatenate([u[:, :, :cw], u[:, :, cw:]], axis=1).reshape(grp, cw)
        r = jnp.dot(ga_ref[b], xg, preferred_element_type=F32)
        y_ref[pl.ds(b, 1), pl.ds(c, 1)] = r.astype(BF16).reshape(1, 1, NR, 2, NJ, cw)
        return carry

    lax.fori_loop(0, NR * NR, stage_a, 0, unroll=FFT_UNROLL)

    def stage_b(g, carry):
        c, ka = g // NR, g % NR
        xg = y_ref[:, pl.ds(c, 1), pl.ds(ka, 1), :, :, :].reshape(grp, cw)
        r = jnp.dot(gb_ref[c], xg, preferred_element_type=F32)
        y2_ref[:, pl.ds(c, 1), pl.ds(ka, 1), :, :, :] = r.astype(BF16).reshape(NR, 1, 1, 2, NJ, cw)
        return carry

    lax.fori_loop(0, NR * NR, stage_b, 0, unroll=FFT_UNROLL)

    def stage_c(g, carry):
        kb, ka = g // NR, g % NR
        xg = y2_ref[pl.ds(kb, 1), :, pl.ds(ka, 1), :, :, :].reshape(grp, cw)
        r = jnp.dot(gc_ref[ka], xg, preferred_element_type=F32)
        y_ref[pl.ds(kb, 1), :, pl.ds(ka, 1), :, :, :] = r.astype(BF16).reshape(1, NR, 1, 2, NJ, cw)
        return carry

    lax.fori_loop(0, NR * NR, stage_c, 0, unroll=FFT_UNROLL)

    def stage_j(g, carry):
        m, kc = g // NR, g % NR
        xg = y_ref[pl.ds(2 * m, 2), pl.ds(kc, 1)].reshape(2 * grp, cw)
        r = jnp.dot(gj_ref[...], xg, preferred_element_type=F32).astype(BF16)
        z_ref[0, :, pl.ds(kc, 1), pl.ds(m, 1), :, 0:cw] = r[:grp].reshape(NJ, 1, 1, 2 * NR, cw)
        z_ref[0, :, pl.ds(kc, 1), pl.ds(m, 1), :, cw:] = r[grp:].reshape(NJ, 1, 1, 2 * NR, cw)
        return carry

    lax.fori_loop(0, NR * NR // 2, stage_j, 0, unroll=FFT_UNROLL // 2)


def _seq_dft(u):
    B, S, W = u.shape
    mats = [jnp.asarray(g).astype(BF16) for g in _dft_stage_matrices()]
    u6 = u.reshape(B, NR, NR, NR, NJ, W)
    const = lambda shp: pl.BlockSpec(shp, lambda b: (0,) * len(shp), pipeline_mode=pl.Buffered(1))
    z = pl.pallas_call(
        _fft_kernel,
        out_shape=jax.ShapeDtypeStruct((B, NJ, NR, NR // 2, 2 * NR, W), BF16),
        grid=(B,),
        in_specs=[pl.BlockSpec((1, NR, NR, NR, NJ, W), lambda b: (b, 0, 0, 0, 0, 0))] + [const(m.shape) for m in mats],
        out_specs=pl.BlockSpec((1, NJ, NR, NR // 2, 2 * NR, W), lambda b: (b, 0, 0, 0, 0, 0)),
        scratch_shapes=[pltpu.VMEM((NR, NR, NR, 2, NJ, FFT_CW), BF16)] * 2,
        compiler_params=pltpu.CompilerParams(dimension_semantics=("arbitrary",), vmem_limit_bytes=VMEM_LIMIT),
        name="seq_dft",
    )(u6, *mats)
    return z.reshape(B, S, W)


def _fold_kernel(cd_ref, sd_ref, wf_ref, m_ref, n_ref):
    m_ref[...] = jnp.zeros(m_ref.shape, m_ref.dtype)
    n_ref[...] = jnp.zeros(n_ref.shape, n_ref.dtype)
    for g in range(N_SLOTS):
        h, o = divmod(g, N_SLOTS // 2)
        w = wf_ref[g]
        blk = slice(o * HEAD_DIM, (o + 1) * HEAD_DIM)
        m_ref[h, blk, blk] = (0.5 * jnp.dot(cd_ref[...], w, preferred_element_type=F32,
                                            precision=lax.Precision.HIGHEST)).astype(BF16)
        n_ref[h, blk, blk] = (0.5 * jnp.dot(sd_ref[...], w, preferred_element_type=F32,
                                            precision=lax.Precision.HIGHEST)).astype(BF16)


def _fold_fourier_weights(w_fourier):
    cd, sd = _head_dim_dft()
    shp = jax.ShapeDtypeStruct((2, FFT_CW, FFT_CW), BF16)
    return pl.pallas_call(_fold_kernel, out_shape=[shp, shp], name="fold_fourier")(
        jnp.asarray(cd), jnp.asarray(sd), w_fourier.astype(F32))


FINAL_TM = 1024
PERM_ROWS = 256


def _final_kernel(x_ref, z_ref, zm_ref, zfirst_ref, gf_ref, ga_ref, o0_ref, o1_ref, o2_ref, l0_ref, l1_ref, l2_ref,
                  w1_ref, w2_ref, ex_ref, p1_ref, p2_ref, j1_ref, wo_ref, y_ref, l1_nat, l2_nat):
    tm = FINAL_TM
    n_sub = tm // PERM_ROWS
    n_sub_seq = zfirst_ref.shape[1]
    cw = FFT_CW
    first_row = lax.broadcasted_iota(jnp.int32, (PERM_ROWS, GROUP_W), 0) == 0
    slot_lane = lax.broadcasted_iota(jnp.int32, (tm, LANES), 1) < N_SLOTS

    for l_ref, l_nat, d in ((l1_ref, l1_nat, DILATIONS[1]), (l2_ref, l2_nat, DILATIONS[2])):
        for r in range(d):
            l_nat[pl.ds(r, tm // d, stride=d), :] = l_ref[0, r]

    zms = []
    for s in range(n_sub):
        oth = zm_ref[0, (n_sub - 1 - s) * PERM_ROWS:(n_sub - s) * PERM_ROWS, :]
        g = pl.program_id(1) * n_sub + s
        fix = zfirst_ref[0, pl.ds(lax.rem(n_sub_seq - g, n_sub_seq), 1), 0:1, :].reshape(1, GROUP_W).astype(F32)
        zms.append(jnp.where(first_row, fix, jnp.dot(j1_ref[...], oth, preferred_element_type=F32)))
    own = z_ref[0].astype(F32)
    zm = jnp.concatenate(zms, axis=0)
    zr, zi, mr, mi = own[:, :cw], own[:, cw:], zm[:, :cw], zm[:, cw:]
    halves = (((zr + mr), (zi - mi)), ((zi + mi), (mr - zr)))
    f = jnp.concatenate(
        [jnp.dot(re.astype(BF16), w1_ref[h], preferred_element_type=F32)
         + jnp.dot(im.astype(BF16), w2_ref[h], preferred_element_type=F32)
         for h, (re, im) in enumerate(halves)], axis=-1)
    gf = gf_ref[0].astype(F32)
    yf = (f * (gf * jax.nn.sigmoid(gf))).astype(BF16)

    o_groups = [o0_ref[0].astype(F32)]
    for o_ref, p_ref, d in ((o1_ref, p1_ref, DILATIONS[1]), (o2_ref, p2_ref, DILATIONS[2])):
        n = PERM_ROWS // d
        o_groups.append(jnp.concatenate(
            [jnp.dot(p_ref[...], jnp.concatenate([o_ref[0, r, s * n:(s + 1) * n, :] for r in range(d)], axis=0),
                     preferred_element_type=F32) for s in range(n_sub)], axis=0))
    sts = (l0_ref[0], l1_nat[...], l2_nat[...])
    mx = jnp.maximum(jnp.maximum(sts[0], sts[1]), sts[2])
    es = [jnp.exp2(st - mx) for st in sts]
    den = None
    for e_c, st in zip(es, sts):
        term = e_c * pltpu.roll(st, LANES - N_SLOTS, 1)
        den = term if den is None else den + term
    inv = 1.0 / jnp.where(slot_lane, den, 1.0)
    o = None
    for e_c, o_c in zip(es, o_groups):
        alpha = jnp.dot((e_c * inv).astype(BF16), ex_ref[...], preferred_element_type=F32)
        o = alpha * o_c if o is None else o + alpha * o_c
    ga = ga_ref[0].astype(F32)
    ya = (o * (ga * jax.nn.sigmoid(ga))).astype(BF16)

    mixed = jnp.dot(jnp.concatenate([yf, ya], axis=-1), wo_ref[...], preferred_element_type=F32)
    y_ref[0] = x_ref[0] + mixed


def _final(x, z, gf, ga, o_list, lse_list, w1, w2, w_out):
    B, S, D = x.shape
    tm = FINAL_TM
    nblk = S // tm
    n_sub_seq = S // PERM_ROWS
    d1, d2 = DILATIONS[1], DILATIONS[2]
    z_first = z.reshape(B, n_sub_seq, PERM_ROWS, GROUP_W)
    nat = lambda w: pl.BlockSpec((1, tm, w), lambda b, i: (b, i, 0))
    perm = lambda d, w: pl.BlockSpec((1, d, tm // d, w), lambda b, i: (b, 0, i, 0))
    const = lambda shp: pl.BlockSpec(shp, lambda b, i: (0,) * len(shp), pipeline_mode=pl.Buffered(1))
    p1 = jnp.asarray(_unpermute_matrix(PERM_ROWS, d1)).astype(BF16)
    p2 = jnp.asarray(_unpermute_matrix(PERM_ROWS, d2)).astype(BF16)
    j1 = jnp.asarray(_mirror_matrix(PERM_ROWS)).astype(BF16)
    ex = jnp.asarray(_slot_expand_matrix()).astype(BF16)
    return pl.pallas_call(
        _final_kernel,
        out_shape=jax.ShapeDtypeStruct((B, S, D), F32),
        grid=(B, nblk),
        in_specs=[nat(D), nat(GROUP_W), pl.BlockSpec((1, tm, GROUP_W), lambda b, i: (b, nblk - 1 - i, 0)),
                  pl.BlockSpec((1, n_sub_seq, 16, GROUP_W), lambda b, i: (b, 0, 0, 0)),
                  nat(GROUP_W), nat(GROUP_W),
                  nat(GROUP_W), perm(d1, GROUP_W), perm(d2, GROUP_W), nat(LANES), perm(d1, LANES), perm(d2, LANES),
                  const((2, FFT_CW, FFT_CW)), const((2, FFT_CW, FFT_CW)), const((LANES, GROUP_W)),
                  const((PERM_ROWS, PERM_ROWS)), const((PERM_ROWS, PERM_ROWS)), const((PERM_ROWS, PERM_ROWS)),
                  const((D, D))],
        out_specs=nat(D),
        scratch_shapes=[pltpu.VMEM((tm, LANES), F32), pltpu.VMEM((tm, LANES), F32)],
        compiler_params=pltpu.CompilerParams(
            dimension_semantics=("arbitrary", "arbitrary"), vmem_limit_bytes=VMEM_LIMIT),
        name="final_mix",
    )(x, z, z, z_first, gf, ga, *o_list, *lse_list, w1, w2, ex, p1, p2, j1, w_out)


def kernel(x, norm_w, w_in, q_norm_w, k_norm_w, w_fourier, w_out):
    B, S, D = x.shape
    n_cfg = len(DILATIONS)
    xf = x.astype(F32)
    gains = jnp.concatenate([q_norm_w.astype(F32).reshape(n_cfg, GROUP_W) * (HEAD_DIM ** -0.5 * LOG2E),
                             k_norm_w.astype(F32).reshape(n_cfg, GROUP_W)], axis=0)
    proj = _project(xf, norm_w.astype(F32), w_in.astype(BF16), gains)
    u_f, g_f, g_a = proj[:3]

    outs, lses = [], []
    for c, d in enumerate(DILATIONS):
        q, k, v = (t.reshape(B * d, S // d, GROUP_W) for t in proj[3 + 3 * c:6 + 3 * c])
        o, lse = _attention(q, k, v, d)
        if d == 1:
            outs.append(o)
            lses.append(lse)
        else:
            outs.append(o.reshape(B, d, S // d, GROUP_W))
            lses.append(lse.reshape(B, d, S // d, LANES))

    z = _seq_dft(u_f)
    w1, w2 = _fold_fourier_weights(w_fourier)
    y = _final(xf, z, g_f, g_a, outs, lses, w1, w2, w_out.astype(BF16))
    return y.astype(x.dtype)
```

```python
import functools

import numpy as np
import jax
import jax.numpy as jnp
from jax import lax
from jax.experimental import pallas as pl
from jax.experimental.pallas import tpu as pltpu

D_MODEL = 1024
HEAD_DIM = 64
N_SLOTS = 8
GROUP_W = N_SLOTS * HEAD_DIM
DILATIONS = (1, 4, 16)
RADIUS = 64
NORM_EPS = 1e-6
MASK_VALUE = -1e30
LOG2E = float(np.log2(np.e))
SEQ = 8192

NR, NJ = 8, 16

LANES = 128
V7X_VMEM_BYTES = 64 * 1024 * 1024
VMEM_LIMIT = V7X_VMEM_BYTES - 4 * 1024 * 1024

BF16 = jnp.bfloat16
F32 = jnp.float32


@functools.lru_cache(maxsize=None)
def _dft_stage_matrices():
    r = np.arange(NR, dtype=np.float64)
    j = np.arange(NJ, dtype=np.float64)
    eye_j = (j[:, None] == j[None, :]).astype(np.float64)[None, :, None, :]

    def radix_stage(phase):
        ph = 2 * np.pi * np.broadcast_to(phase, (NR, NJ, NR, NJ))
        c, s = np.cos(ph) * eye_j, np.sin(ph) * eye_j
        g = np.zeros((NR, 2, NJ, NR, 2, NJ))
        g[:, 0, :, :, 0, :] = c
        g[:, 0, :, :, 1, :] = s
        g[:, 1, :, :, 0, :] = -s
        g[:, 1, :, :, 1, :] = c
        return g.reshape(2 * NR * NJ, 2 * NR * NJ)

    kk = r[:, None, None, None]
    dig = r[None, None, :, None]
    jj = j[None, :, None, None]
    n_total = NJ * NR ** 3
    ga = np.stack([radix_stage(dig * kk / NR + bv * kk / NR ** 2 + jj * kk / n_total) for bv in range(NR)])
    gb = np.stack([radix_stage(dig * kk / NR + cv * kk / NR ** 2 + jj * kk / (NJ * NR ** 2)) for cv in range(NR)])
    gc = np.stack([radix_stage(dig * kk / NR + jj * kk / (NJ * NR) + dig * kav / NR ** 3) for kav in range(NR)])

    ph = 2 * np.pi * j[None, :] * j[:, None] / NJ
    c, s = np.cos(ph), np.sin(ph)
    gj = np.zeros((2, NJ, 2, NR, 2, NR, 2, NJ))
    for kb in range(2):
        for kav in range(NR):
            gj[0, :, kb, kav, kb, kav, 0, :] = c
            gj[0, :, kb, kav, kb, kav, 1, :] = s
            gj[1, :, kb, kav, kb, kav, 0, :] = -s
            gj[1, :, kb, kav, kb, kav, 1, :] = c
    gj = gj.reshape(4 * NR * NJ, 4 * NR * NJ)
    return tuple(np.asarray(g, dtype=np.float32) for g in (ga, gb, gc, gj))


@functools.lru_cache(maxsize=None)
def _mirror_matrix(rows):
    t = np.arange(rows)
    return np.asarray((t[:, None] + t[None, :] == rows) & (t[:, None] > 0), dtype=np.float32)


@functools.lru_cache(maxsize=None)
def _head_dim_dft():
    m = np.arange(HEAD_DIM, dtype=np.float64)
    ang = 2 * np.pi * m[:, None] * m[None, :] / HEAD_DIM
    scale = 1.0 / np.sqrt(SEQ * HEAD_DIM)
    return (np.asarray(np.cos(ang) * scale, dtype=np.float32), np.asarray(np.sin(ang) * scale, dtype=np.float32))


@functools.lru_cache(maxsize=None)
def _attn_bias(dilation):
    slopes = 2.0 ** (-8.0 * np.arange(1, N_SLOTS + 1, dtype=np.float64) / N_SLOTS)
    a = np.arange(2 * RADIUS)[:, None]
    c = np.arange(4 * RADIUS)[None, :]
    rel = np.abs(c - RADIUS - a)
    key_ok = (np.ones_like(c, dtype=bool), c >= RADIUS, c < 3 * RADIUS)
    tab = np.zeros((3, N_SLOTS, 2 * RADIUS, 4 * RADIUS))
    for var in range(3):
        for s in range(N_SLOTS):
            tab[var, s] = np.where((rel <= RADIUS) & key_ok[var], -slopes[s] * rel * dilation * LOG2E, MASK_VALUE)
    return np.asarray(tab.reshape(3, N_SLOTS // 2, 4 * RADIUS, 4 * RADIUS), dtype=np.float32)


@functools.lru_cache(maxsize=None)
def _slot_expand_matrix():
    r = np.arange(LANES)[:, None]
    c = np.arange(GROUP_W)[None, :]
    return np.asarray((r == c // HEAD_DIM), dtype=np.float32)


@functools.lru_cache(maxsize=None)
def _unpermute_matrix(tm, dilation):
    t = np.arange(tm)
    src = (t % dilation) * (tm // dilation) + t // dilation
    return np.asarray(src[:, None] == np.arange(tm)[None, :], dtype=np.float32)


PROJ_TM = 1024
COL_UF, COL_GF, COL_Q, COL_K, COL_V, COL_GA = 0, 1, 2, 5, 8, 11


def _proj_kernel(x_ref, nw_ref, w_ref, gain_ref, uf_ref, gf_ref, ga_ref, *rest):
    qkv_refs, (xw_ref, rs_ref, lvl_x_ref, lvl_rs_ref) = rest[:-4], rest[-4:]
    x = x_ref[0]
    xw = x * nw_ref[...]
    ms = jnp.mean(x * x, axis=-1, keepdims=True) + NORM_EPS
    n_slab = xw_ref.shape[0]
    for s in range(n_slab):
        xw_ref[s] = xw[:, s * LANES:(s + 1) * LANES]
    rs_ref[0] = jnp.broadcast_to(lax.rsqrt(ms), (PROJ_TM, LANES))
    rs_ref[1] = jnp.broadcast_to(NORM_EPS * ms, (PROJ_TM, LANES))
    low = lax.broadcasted_iota(jnp.int32, (PROJ_TM, LANES), 1) < HEAD_DIM

    def project(lhs, row_scale, head_eps, col, gain_row):
        acc = jnp.dot(lhs, w_ref[:, col * GROUP_W:(col + 1) * GROUP_W], preferred_element_type=F32)
        tiles = []
        for t in range(GROUP_W // LANES):
            a = acc[:, t * LANES:(t + 1) * LANES]
            if gain_row is None:
                tiles.append(a * row_scale)
                continue
            z = a * a
            z_lo = jnp.where(low, z, 0.0)
            ms_lo = jnp.sum(z_lo, axis=-1, keepdims=True) * (1.0 / HEAD_DIM)
            ms_hi = jnp.sum(z - z_lo, axis=-1, keepdims=True) * (1.0 / HEAD_DIM)
            tiles.append(a * lax.rsqrt(jnp.where(low, ms_lo, ms_hi) + head_eps))
        out = jnp.concatenate(tiles, axis=-1)
        if gain_row is not None:
            out = out * gain_ref[gain_row:gain_row + 1, :]
        return out.astype(BF16)

    lhs = xw.astype(BF16)
    row_scale, head_eps = rs_ref[0], rs_ref[1]
    uf_ref[0] = project(lhs, row_scale, head_eps, COL_UF, None)
    gf_ref[0] = project(lhs, row_scale, head_eps, COL_GF, None)
    ga_ref[0] = project(lhs, row_scale, head_eps, COL_GA, None)
    residues, src_x, src_rs, d_prev = [0], xw_ref, rs_ref, 1
    for c, d in enumerate(DILATIONS):
        q_ref, k_ref, v_ref = qkv_refs[3 * c:3 * c + 3]
        rows = PROJ_TM // d
        if d > 1:
            step, rows_prev, n_prev = d // d_prev, PROJ_TM // d_prev, len(residues)

            def split(ref, idx):
                return jnp.concatenate([ref[idx, pl.ds(p * rows_prev + r, rows, stride=step), :]
                                        for p in range(n_prev) for r in range(step)], axis=0)

            x32 = [split(src_x, s) for s in range(n_slab)]
            row_scale, head_eps = split(src_rs, 0), split(src_rs, 1)
            residues = [r * d_prev + res for res in residues for r in range(step)]
            if d != DILATIONS[-1]:
                for s in range(n_slab):
                    lvl_x_ref[s] = x32[s]
                lvl_rs_ref[0], lvl_rs_ref[1] = row_scale, head_eps
                src_x, src_rs = lvl_x_ref, lvl_rs_ref
            lhs = jnp.concatenate(x32, axis=1).astype(BF16)
            d_prev = d
        for o_ref, col, gain_row in ((q_ref, COL_Q + c, c), (k_ref, COL_K + c, 3 + c), (v_ref, COL_V + c, None)):
            res = project(lhs, row_scale, head_eps, col, gain_row)
            for q, resid in enumerate(residues):
                o_ref[0, resid] = res[q * rows:(q + 1) * rows]


def _project(x, norm_w, w_in, gains):
    B, S, D = x.shape
    tm = PROJ_TM
    nat = jax.ShapeDtypeStruct((B, S, GROUP_W), BF16)
    nat_spec = pl.BlockSpec((1, tm, GROUP_W), lambda b, i: (b, i, 0))
    out_shape, out_specs = [nat] * 3, [nat_spec] * 3
    for d in DILATIONS:
        out_shape += [jax.ShapeDtypeStruct((B, d, S // d, GROUP_W), BF16)] * 3
        out_specs += [pl.BlockSpec((1, d, tm // d, GROUP_W), lambda b, i: (b, 0, i, 0))] * 3
    const = lambda shp: pl.BlockSpec(shp, lambda b, i: (0,) * len(shp), pipeline_mode=pl.Buffered(1))
    return pl.pallas_call(
        _proj_kernel,
        out_shape=out_shape,
        grid=(B, S // tm),
        in_specs=[pl.BlockSpec((1, tm, D), lambda b, i: (b, i, 0)),
                  const((1, D)), const(w_in.shape), const(gains.shape)],
        out_specs=out_specs,
        scratch_shapes=[pltpu.VMEM((D // LANES, tm, LANES), F32), pltpu.VMEM((2, tm, LANES), F32)] * 2,
        compiler_params=pltpu.CompilerParams(
            dimension_semantics=("arbitrary", "arbitrary"), vmem_limit_bytes=VMEM_LIMIT),
        name="proj",
    )(x, norm_w.reshape(1, D), w_in, gains)


ATT_STEP_ROWS = 4096
ATT_SUB = 2 * RADIUS


def _attn_kernel(q_ref, kp_ref, kc_ref, kn_ref, vp_ref, vc_ref, vn_ref, bias_ref, o_ref, st_ref, kbuf, vbuf):
    nb, ti = q_ref.shape[0], q_ref.shape[1]
    i = pl.program_id(1)
    first = i == 0
    last = i == pl.num_programs(1) - 1
    lane = lax.broadcasted_iota(jnp.int32, (ATT_SUB, LANES), 1)
    low = lane < HEAD_DIM
    ones = jnp.ones((2 * ATT_SUB, LANES), BF16)
    n_sub = ti // ATT_SUB
    for b in range(nb):
        kbuf[b, 0:RADIUS] = kp_ref[b]
        kbuf[b, RADIUS:RADIUS + ti] = kc_ref[b]
        kbuf[b, RADIUS + ti:] = kn_ref[b]
        vbuf[b, 0:RADIUS] = vp_ref[b]
        vbuf[b, RADIUS:RADIUS + ti] = vc_ref[b]
        vbuf[b, RADIUS + ti:] = vn_ref[b]
        for j in range(n_sub):
            r0 = j * ATT_SUB
            variant = 0
            if j == 0:
                variant = jnp.where(first, 1, 0)
            if j == n_sub - 1:
                variant = jnp.where(last, 2, 0)
            st = jnp.zeros((ATT_SUB, LANES), F32)
            for p in range(N_SLOTS // 2):
                c0 = p * LANES
                qp = q_ref[b, r0:r0 + ATT_SUB, c0:c0 + LANES]
                kw = kbuf[b, r0:r0 + 2 * ATT_SUB, c0:c0 + LANES]
                vw = jnp.concatenate([vbuf[b, r0:r0 + 2 * ATT_SUB, c0:c0 + LANES], ones], axis=1)
                zero = jnp.zeros_like(qp)
                qs = jnp.concatenate([jnp.where(low, qp, zero), jnp.where(low, zero, qp)], axis=0)
                s = lax.dot_general(qs, kw, (((1,), (1,)), ((), ())), preferred_element_type=F32)
                s = s + bias_ref[variant, p]
                m = jnp.max(s, axis=-1, keepdims=True)
                e = jnp.exp2(s - m).astype(BF16)
                ol = jnp.dot(e, vw, preferred_element_type=F32)
                o_ref[b, r0:r0 + ATT_SUB, c0:c0 + LANES] = jnp.where(
                    low, ol[:ATT_SUB, :LANES], ol[ATT_SUB:, :LANES]).astype(BF16)
                st = jnp.where(lane == 2 * p, m[:ATT_SUB], st)
                st = jnp.where(lane == 2 * p + 1, m[ATT_SUB:], st)
                st = jnp.where(lane == N_SLOTS + 2 * p, ol[:ATT_SUB, LANES:], st)
                st = jnp.where(lane == N_SLOTS + 2 * p + 1, ol[ATT_SUB:, LANES:], st)
            st_ref[b, r0:r0 + ATT_SUB, :] = st


def _attention(q, k, v, dilation):
    N, L, W = q.shape
    ti = min(L, ATT_STEP_ROWS)
    nb = ATT_STEP_ROWS // ti
    nblk = L // ti
    halo_per_blk = ti // RADIUS
    n_halo = L // RADIUS
    bias = jnp.asarray(_attn_bias(dilation))
    cur = pl.BlockSpec((nb, ti, W), lambda n, i: (n, i, 0))
    prev = pl.BlockSpec((nb, RADIUS, W), lambda n, i: (n, jnp.maximum(i * halo_per_blk - 1, 0), 0))
    nxt = pl.BlockSpec((nb, RADIUS, W), lambda n, i: (n, jnp.minimum((i + 1) * halo_per_blk, n_halo - 1), 0))
    return pl.pallas_call(
        _attn_kernel,
        out_shape=[jax.ShapeDtypeStruct((N, L, W), BF16), jax.ShapeDtypeStruct((N, L, LANES), F32)],
        grid=(N // nb, nblk),
        in_specs=[cur, prev, cur, nxt, prev, cur, nxt,
                  pl.BlockSpec(bias.shape, lambda n, i: (0, 0, 0, 0), pipeline_mode=pl.Buffered(1))],
        out_specs=[pl.BlockSpec((nb, ti, W), lambda n, i: (n, i, 0)),
                   pl.BlockSpec((nb, ti, LANES), lambda n, i: (n, i, 0))],
        scratch_shapes=[pltpu.VMEM((nb, ti + 2 * RADIUS, W), BF16), pltpu.VMEM((nb, ti + 2 * RADIUS, W), BF16)],
        compiler_params=pltpu.CompilerParams(
            dimension_semantics=("arbitrary", "arbitrary"), vmem_limit_bytes=VMEM_LIMIT),
        name=f"attn_d{dilation}",
    )(q, k, k, k, v, v, v, bias)


FFT_CW = GROUP_W // 2
FFT_UNROLL = NR * NR


def _fft_kernel(u_ref, ga_ref, gb_ref, gc_ref, gj_ref, z_ref, y_ref, y2_ref):
    cw = FFT_CW
    grp = NR * 2 * NJ

    def stage_a(g, carry):
        b, c = g // NR, g % NR
        u = u_ref[0, :, pl.ds(b, 1), pl.ds(c, 1), :, :].reshape(NR, NJ, 2 * cw)
        xg = jnp.concatenate([u[:, :, :cw], u[:, :, cw:]], axis=1).reshape(grp, cw)
        r = jnp.dot(ga_ref[b], xg, preferred_element_type=F32)
        y_ref[pl.ds(b, 1), pl.ds(c, 1)] = r.astype(BF16).reshape(1, 1, NR, 2, NJ, cw)
        return carry

    lax.fori_loop(0, NR * NR, stage_a, 0, unroll=FFT_UNROLL)

    def stage_b(g, carry):
        c, ka = g // NR, g % NR
        xg = y_ref[:, pl.ds(c, 1), pl.ds(ka, 1), :, :, :].reshape(grp, cw)
        r = jnp.dot(gb_ref[c], xg, preferred_element_type=F32)
        y2_ref[:, pl.ds(c, 1), pl.ds(ka, 1), :, :, :] = r.astype(BF16).reshape(NR, 1, 1, 2, NJ, cw)
        return carry

    lax.fori_loop(0, NR * NR, stage_b, 0, unroll=FFT_UNROLL)

    def stage_c(g, carry):
        kb, ka = g // NR, g % NR
        xg = y2_ref[pl.ds(kb, 1), :, pl.ds(ka, 1), :, :, :].reshape(grp, cw)
        r = jnp.dot(gc_ref[ka], xg, preferred_element_type=F32)
        y_ref[pl.ds(kb, 1), :, pl.ds(ka, 1), :, :, :] = r.astype(BF16).reshape(1, NR, 1, 2, NJ, cw)
        return carry

    lax.fori_loop(0, NR * NR, stage_c, 0, unroll=FFT_UNROLL)

    def stage_j(g, carry):
        m, kc = g // NR, g % NR
        xg = y_ref[pl.ds(2 * m, 2), pl.ds(kc, 1)].reshape(2 * grp, cw)
        r = jnp.dot(gj_ref[...], xg, preferred_element_type=F32).astype(BF16)
        z_ref[0, :, pl.ds(kc, 1), pl.ds(m, 1), :, 0:cw] = r[:grp].reshape(NJ, 1, 1, 2 * NR, cw)
        z_ref[0, :, pl.ds(kc, 1), pl.ds(m, 1), :, cw:] = r[grp:].reshape(NJ, 1, 1, 2 * NR, cw)
        return carry

    lax.fori_loop(0, NR * NR // 2, stage_j, 0, unroll=FFT_UNROLL // 2)


def _seq_dft(u):
    B, S, W = u.shape
    mats = [jnp.asarray(g).astype(BF16) for g in _dft_stage_matrices()]
    u6 = u.reshape(B, NR, NR, NR, NJ, W)
    const = lambda shp: pl.BlockSpec(shp, lambda b: (0,) * len(shp), pipeline_mode=pl.Buffered(1))
    z = pl.pallas_call(
        _fft_kernel,
        out_shape=jax.ShapeDtypeStruct((B, NJ, NR, NR // 2, 2 * NR, W), BF16),
        grid=(B,),
        in_specs=[pl.BlockSpec((1, NR, NR, NR, NJ, W), lambda b: (b, 0, 0, 0, 0, 0))] + [const(m.shape) for m in mats],
        out_specs=pl.BlockSpec((1, NJ, NR, NR // 2, 2 * NR, W), lambda b: (b, 0, 0, 0, 0, 0)),
        scratch_shapes=[pltpu.VMEM((NR, NR, NR, 2, NJ, FFT_CW), BF16)] * 2,
        compiler_params=pltpu.CompilerParams(dimension_semantics=("arbitrary",), vmem_limit_bytes=VMEM_LIMIT),
        name="seq_dft",
    )(u6, *mats)
    return z.reshape(B, S, W)


def _fold_kernel(cd_ref, sd_ref, wf_ref, m_ref, n_ref):
    m_ref[...] = jnp.zeros(m_ref.shape, m_ref.dtype)
    n_ref[...] = jnp.zeros(n_ref.shape, n_ref.dtype)
    for g in range(N_SLOTS):
        h, o = divmod(g, N_SLOTS // 2)
        w = wf_ref[g]
        blk = slice(o * HEAD_DIM, (o + 1) * HEAD_DIM)
        m_ref[h, blk, blk] = (0.5 * jnp.dot(cd_ref[...], w, preferred_element_type=F32,
                                            precision=lax.Precision.HIGHEST)).astype(BF16)
        n_ref[h, blk, blk] = (0.5 * jnp.dot(sd_ref[...], w, preferred_element_type=F32,
                                            precision=lax.Precision.HIGHEST)).astype(BF16)


def _fold_fourier_weights(w_fourier):
    cd, sd = _head_dim_dft()
    shp = jax.ShapeDtypeStruct((2, FFT_CW, FFT_CW), BF16)
    return pl.pallas_call(_fold_kernel, out_shape=[shp, shp], name="fold_fourier")(
        jnp.asarray(cd), jnp.asarray(sd), w_fourier.astype(F32))


FINAL_TM = 1024
PERM_ROWS = 256


def _final_kernel(x_ref, z_ref, zm_ref, zfirst_ref, gf_ref, ga_ref, o0_ref, o1_ref, o2_ref, l0_ref, l1_ref, l2_ref,
                  w1_ref, w2_ref, ex_ref, p1_ref, p2_ref, j1_ref, wo_ref, y_ref, l1_nat, l2_nat):
    tm = FINAL_TM
    n_sub = tm // PERM_ROWS
    n_sub_seq = zfirst_ref.shape[1]
    cw = FFT_CW
    first_row = lax.broadcasted_iota(jnp.int32, (PERM_ROWS, GROUP_W), 0) == 0
    slot_lane = lax.broadcasted_iota(jnp.int32, (tm, LANES), 1) < N_SLOTS

    for l_ref, l_nat, d in ((l1_ref, l1_nat, DILATIONS[1]), (l2_ref, l2_nat, DILATIONS[2])):
        for r in range(d):
            l_nat[pl.ds(r, tm // d, stride=d), :] = l_ref[0, r]

    zms = []
    for s in range(n_sub):
        oth = zm_ref[0, (n_sub - 1 - s) * PERM_ROWS:(n_sub - s) * PERM_ROWS, :]
        g = pl.program_id(1) * n_sub + s
        fix = zfirst_ref[0, pl.ds(lax.rem(n_sub_seq - g, n_sub_seq), 1), 0:1, :].reshape(1, GROUP_W).astype(F32)
        zms.append(jnp.where(first_row, fix, jnp.dot(j1_ref[...], oth, preferred_element_type=F32)))
    own = z_ref[0].astype(F32)
    zm = jnp.concatenate(zms, axis=0)
    zr, zi, mr, mi = own[:, :cw], own[:, cw:], zm[:, :cw], zm[:, cw:]
    halves = (((zr + mr), (zi - mi)), ((zi + mi), (mr - zr)))
    f = jnp.concatenate(
        [jnp.dot(re.astype(BF16), w1_ref[h], preferred_element_type=F32)
         + jnp.dot(im.astype(BF16), w2_ref[h], preferred_element_type=F32)
         for h, (re, im) in enumerate(halves)], axis=-1)
    gf = gf_ref[0].astype(F32)
    yf = (f * (gf * jax.nn.sigmoid(gf))).astype(BF16)

    o_groups = [o0_ref[0].astype(F32)]
    for o_ref, p_ref, d in ((o1_ref, p1_ref, DILATIONS[1]), (o2_ref, p2_ref, DILATIONS[2])):
        n = PERM_ROWS // d
        o_groups.append(jnp.concatenate(
            [jnp.dot(p_ref[...], jnp.concatenate([o_ref[0, r, s * n:(s + 1) * n, :] for r in range(d)], axis=0),
                     preferred_element_type=F32) for s in range(n_sub)], axis=0))
    sts = (l0_ref[0], l1_nat[...], l2_nat[...])
    mx = jnp.maximum(jnp.maximum(sts[0], sts[1]), sts[2])
    es = [jnp.exp2(st - mx) for st in sts]
    den = None
    for e_c, st in zip(es, sts):
        term = e_c * pltpu.roll(st, LANES - N_SLOTS, 1)
        den = term if den is None else den + term
    inv = 1.0 / jnp.where(slot_lane, den, 1.0)
    o = None
    for e_c, o_c in zip(es, o_groups):
        alpha = jnp.dot((e_c * inv).astype(BF16), ex_ref[...], preferred_element_type=F32)
        o = alpha * o_c if o is None else o + alpha * o_c
    ga = ga_ref[0].astype(F32)
    ya = (o * (ga * jax.nn.sigmoid(ga))).astype(BF16)

    mixed = jnp.dot(jnp.concatenate([yf, ya], axis=-1), wo_ref[...], preferred_element_type=F32)
    y_ref[0] = x_ref[0] + mixed


def _final(x, z, gf, ga, o_list, lse_list, w1, w2, w_out):
    B, S, D = x.shape
    tm = FINAL_TM
    nblk = S // tm
    n_sub_seq = S // PERM_ROWS
    d1, d2 = DILATIONS[1], DILATIONS[2]
    z_first = z.reshape(B, n_sub_seq, PERM_ROWS, GROUP_W)
    nat = lambda w: pl.BlockSpec((1, tm, w), lambda b, i: (b, i, 0))
    perm = lambda d, w: pl.BlockSpec((1, d, tm // d, w), lambda b, i: (b, 0, i, 0))
    const = lambda shp: pl.BlockSpec(shp, lambda b, i: (0,) * len(shp), pipeline_mode=pl.Buffered(1))
    p1 = jnp.asarray(_unpermute_matrix(PERM_ROWS, d1)).astype(BF16)
    p2 = jnp.asarray(_unpermute_matrix(PERM_ROWS, d2)).astype(BF16)
    j1 = jnp.asarray(_mirror_matrix(PERM_ROWS)).astype(BF16)
    ex = jnp.asarray(_slot_expand_matrix()).astype(BF16)
    return pl.pallas_call(
        _final_kernel,
        out_shape=jax.ShapeDtypeStruct((B, S, D), F32),
        grid=(B, nblk),
        in_specs=[nat(D), nat(GROUP_W), pl.BlockSpec((1, tm, GROUP_W), lambda b, i: (b, nblk - 1 - i, 0)),
                  pl.BlockSpec((1, n_sub_seq, 16, GROUP_W), lambda b, i: (b, 0, 0, 0)),
                  nat(GROUP_W), nat(GROUP_W),
                  nat(GROUP_W), perm(d1, GROUP_W), perm(d2, GROUP_W), nat(LANES), perm(d1, LANES), perm(d2, LANES),
                  const((2, FFT_CW, FFT_CW)), const((2, FFT_CW, FFT_CW)), const((LANES, GROUP_W)),
                  const((PERM_ROWS, PERM_ROWS)), const((PERM_ROWS, PERM_ROWS)), const((PERM_ROWS, PERM_ROWS)),
                  const((D, D))],
        out_specs=nat(D),
        scratch_shapes=[pltpu.VMEM((tm, LANES), F32), pltpu.VMEM((tm, LANES), F32)],
        compiler_params=pltpu.CompilerParams(
            dimension_semantics=("arbitrary", "arbitrary"), vmem_limit_bytes=VMEM_LIMIT),
        name="final_mix",
    )(x, z, z, z_first, gf, ga, *o_list, *lse_list, w1, w2, ex, p1, p2, j1, w_out)


def kernel(x, norm_w, w_in, q_norm_w, k_norm_w, w_fourier, w_out):
    B, S, D = x.shape
    n_cfg = len(DILATIONS)
    xf = x.astype(F32)
    gains = jnp.concatenate([q_norm_w.astype(F32).reshape(n_cfg, GROUP_W) * (HEAD_DIM ** -0.5 * LOG2E),
                             k_norm_w.astype(F32).reshape(n_cfg, GROUP_W)], axis=0)
    proj = _project(xf, norm_w.astype(F32), w_in.astype(BF16), gains)
    u_f, g_f, g_a = proj[:3]

    outs, lses = [], []
    for c, d in enumerate(DILATIONS):
        q, k, v = (t.reshape(B * d, S // d, GROUP_W) for t in proj[3 + 3 * c:6 + 3 * c])
        o, lse = _attention(q, k, v, d)
        if d == 1:
            outs.append(o)
            lses.append(lse)
        else:
            outs.append(o.reshape(B, d, S // d, GROUP_W))
            lses.append(lse.reshape(B, d, S // d, LANES))

    z = _seq_dft(u_f)
    w1, w2 = _fold_fourier_weights(w_fourier)
    y = _final(xf, z, g_f, g_a, outs, lses, w1, w2, w_out.astype(BF16))
    return y.astype(x.dtype)
```

```python
import functools

import numpy as np
import jax
import jax.numpy as jnp
from jax import lax
from jax.experimental import pallas as pl
from jax.experimental.pallas import tpu as pltpu

D_MODEL = 1024
HEAD_DIM = 64
N_SLOTS = 8
GROUP_W = N_SLOTS * HEAD_DIM
DILATIONS = (1, 4, 16)
RADIUS = 64
NORM_EPS = 1e-6
MASK_VALUE = -1e30
LOG2E = float(np.log2(np.e))
SEQ = 8192

NR, NJ = 8, 16

LANES = 128
V7X_VMEM_BYTES = 64 * 1024 * 1024
VMEM_LIMIT = V7X_VMEM_BYTES - 4 * 1024 * 1024

BF16 = jnp.bfloat16
F32 = jnp.float32


@functools.lru_cache(maxsize=None)
def _dft_stage_matrices():
    r = np.arange(NR, dtype=np.float64)
    j = np.arange(NJ, dtype=np.float64)
    eye_j = (j[:, None] == j[None, :]).astype(np.float64)[None, :, None, :]

    def radix_stage(phase):
        ph = 2 * np.pi * np.broadcast_to(phase, (NR, NJ, NR, NJ))
        c, s = np.cos(ph) * eye_j, np.sin(ph) * eye_j
        g = np.zeros((NR, 2, NJ, NR, 2, NJ))
        g[:, 0, :, :, 0, :] = c
        g[:, 0, :, :, 1, :] = s
        g[:, 1, :, :, 0, :] = -s
        g[:, 1, :, :, 1, :] = c
        return g.reshape(2 * NR * NJ, 2 * NR * NJ)

    kk = r[:, None, None, None]
    dig = r[None, None, :, None]
    jj = j[None, :, None, None]
    n_total = NJ * NR ** 3
    ga = np.stack([radix_stage(dig * kk / NR + bv * kk / NR ** 2 + jj * kk / n_total) for bv in range(NR)])
    gb = np.stack([radix_stage(dig * kk / NR + cv * kk / NR ** 2 + jj * kk / (NJ * NR ** 2)) for cv in range(NR)])
    gc = np.stack([radix_stage(dig * kk / NR + jj * kk / (NJ * NR) + dig * kav / NR ** 3) for kav in range(NR)])

    ph = 2 * np.pi * j[None, :] * j[:, None] / NJ
    c, s = np.cos(ph), np.sin(ph)
    gj = np.zeros((2, NJ, 2, NR, 2, NR, 2, NJ))
    for kb in range(2):
        for kav in range(NR):
            gj[0, :, kb, kav, kb, kav, 0, :] = c
            gj[0, :, kb, kav, kb, kav, 1, :] = s
            gj[1, :, kb, kav, kb, kav, 0, :] = -s
            gj[1, :, kb, kav, kb, kav, 1, :] = c
    gj = gj.reshape(4 * NR * NJ, 4 * NR * NJ)
    return tuple(np.asarray(g, dtype=np.float32) for g in (ga, gb, gc, gj))


@functools.lru_cache(maxsize=None)
def _mirror_matrix(rows):
    t = np.arange(rows)
    return np.asarray((t[:, None] + t[None, :] == rows) & (t[:, None] > 0), dtype=np.float32)


@functools.lru_cache(maxsize=None)
def _head_dim_dft():
    m = np.arange(HEAD_DIM, dtype=np.float64)
    ang = 2 * np.pi * m[:, None] * m[None, :] / HEAD_DIM
    scale = 1.0 / np.sqrt(SEQ * HEAD_DIM)
    return (np.asarray(np.cos(ang) * scale, dtype=np.float32), np.asarray(np.sin(ang) * scale, dtype=np.float32))


@functools.lru_cache(maxsize=None)
def _attn_bias(dilation):
    slopes = 2.0 ** (-8.0 * np.arange(1, N_SLOTS + 1, dtype=np.float64) / N_SLOTS)
    a = np.arange(2 * RADIUS)[:, None]
    c = np.arange(4 * RADIUS)[None, :]
    rel = np.abs(c - RADIUS - a)
    key_ok = (np.ones_like(c, dtype=bool), c >= RADIUS, c < 3 * RADIUS)
    tab = np.zeros((3, N_SLOTS, 2 * RADIUS, 4 * RADIUS))
    for var in range(3):
        for s in range(N_SLOTS):
            tab[var, s] = np.where((rel <= RADIUS) & key_ok[var], -slopes[s] * rel * dilation * LOG2E, MASK_VALUE)
    return np.asarray(tab.reshape(3, N_SLOTS // 2, 4 * RADIUS, 4 * RADIUS), dtype=np.float32)


@functools.lru_cache(maxsize=None)
def _slot_expand_matrix():
    r = np.arange(LANES)[:, None]
    c = np.arange(GROUP_W)[None, :]
    return np.asarray((r == c // HEAD_DIM), dtype=np.float32)


@functools.lru_cache(maxsize=None)
def _unpermute_matrix(tm, dilation):
    t = np.arange(tm)
    src = (t % dilation) * (tm // dilation) + t // dilation
    return np.asarray(src[:, None] == np.arange(tm)[None, :], dtype=np.float32)


PROJ_TM = 1024
COL_UF, COL_GF, COL_Q, COL_K, COL_V, COL_GA = 0, 1, 2, 5, 8, 11


def _proj_kernel(x_ref, nw_ref, w_ref, gain_ref, uf_ref, gf_ref, ga_ref, *rest):
    qkv_refs, (xw_ref, rs_ref, lvl_x_ref, lvl_rs_ref) = rest[:-4], rest[-4:]
    x = x_ref[0]
    xw = x * nw_ref[...]
    ms = jnp.mean(x * x, axis=-1, keepdims=True) + NORM_EPS
    n_slab = xw_ref.shape[0]
    for s in range(n_slab):
        xw_ref[s] = xw[:, s * LANES:(s + 1) * LANES]
    rs_ref[0] = jnp.broadcast_to(lax.rsqrt(ms), (PROJ_TM, LANES))
    rs_ref[1] = jnp.broadcast_to(NORM_EPS * ms, (PROJ_TM, LANES))
    low = lax.broadcasted_iota(jnp.int32, (PROJ_TM, LANES), 1) < HEAD_DIM

    def project(lhs, row_scale, head_eps, col, gain_row):
        acc = jnp.dot(lhs, w_ref[:, col * GROUP_W:(col + 1) * GROUP_W], preferred_element_type=F32)
        tiles = []
        for t in range(GROUP_W // LANES):
            a = acc[:, t * LANES:(t + 1) * LANES]
            if gain_row is None:
                tiles.append(a * row_scale)
                continue
            z = a * a
            z_lo = jnp.where(low, z, 0.0)
            ms_lo = jnp.sum(z_lo, axis=-1, keepdims=True) * (1.0 / HEAD_DIM)
            ms_hi = jnp.sum(z - z_lo, axis=-1, keepdims=True) * (1.0 / HEAD_DIM)
            tiles.append(a * lax.rsqrt(jnp.where(low, ms_lo, ms_hi) + head_eps))
        out = jnp.concatenate(tiles, axis=-1)
        if gain_row is not None:
            out = out * gain_ref[gain_row:gain_row + 1, :]
        return out.astype(BF16)

    lhs = xw.astype(BF16)
    row_scale, head_eps = rs_ref[0], rs_ref[1]
    uf_ref[0] = project(lhs, row_scale, head_eps, COL_UF, None)
    gf_ref[0] = project(lhs, row_scale, head_eps, COL_GF, None)
    ga_ref[0] = project(lhs, row_scale, head_eps, COL_GA, None)
    residues, src_x, src_rs, d_prev = [0], xw_ref, rs_ref, 1
    for c, d in enumerate(DILATIONS):
        q_ref, k_ref, v_ref = qkv_refs[3 * c:3 * c + 3]
        rows = PROJ_TM // d
        if d > 1:
            step, rows_prev, n_prev = d // d_prev, PROJ_TM // d_prev, len(residues)

            def split(ref, idx):
                return jnp.concatenate([ref[idx, pl.ds(p * rows_prev + r, rows, stride=step), :]
                                        for p in range(n_prev) for r in range(step)], axis=0)

            x32 = [split(src_x, s) for s in range(n_slab)]
            row_scale, head_eps = split(src_rs, 0), split(src_rs, 1)
            residues = [r * d_prev + res for res in residues for r in range(step)]
            if d != DILATIONS[-1]:
                for s in range(n_slab):
                    lvl_x_ref[s] = x32[s]
                lvl_rs_ref[0], lvl_rs_ref[1] = row_scale, head_eps
                src_x, src_rs = lvl_x_ref, lvl_rs_ref
            lhs = jnp.concatenate(x32, axis=1).astype(BF16)
            d_prev = d
        for o_ref, col, gain_row in ((q_ref, COL_Q + c, c), (k_ref, COL_K + c, 3 + c), (v_ref, COL_V + c, None)):
            res = project(lhs, row_scale, head_eps, col, gain_row)
            for q, resid in enumerate(residues):
                o_ref[0, resid] = res[q * rows:(q + 1) * rows]


def _project(x, norm_w, w_in, gains):
    B, S, D = x.shape
    tm = PROJ_TM
    nat = jax.ShapeDtypeStruct((B, S, GROUP_W), BF16)
    nat_spec = pl.BlockSpec((1, tm, GROUP_W), lambda b, i: (b, i, 0))
    out_shape, out_specs = [nat] * 3, [nat_spec] * 3
    for d in DILATIONS:
        out_shape += [jax.ShapeDtypeStruct((B, d, S // d, GROUP_W), BF16)] * 3
        out_specs += [pl.BlockSpec((1, d, tm // d, GROUP_W), lambda b, i: (b, 0, i, 0))] * 3
    const = lambda shp: pl.BlockSpec(shp, lambda b, i: (0,) * len(shp), pipeline_mode=pl.Buffered(1))
    return pl.pallas_call(
        _proj_kernel,
        out_shape=out_shape,
        grid=(B, S // tm),
        in_specs=[pl.BlockSpec((1, tm, D), lambda b, i: (b, i, 0)),
                  const((1, D)), const(w_in.shape), const(gains.shape)],
        out_specs=out_specs,
        scratch_shapes=[pltpu.VMEM((D // LANES, tm, LANES), F32), pltpu.VMEM((2, tm, LANES), F32)] * 2,
        compiler_params=pltpu.CompilerParams(
            dimension_semantics=("parallel", "arbitrary"), vmem_limit_bytes=VMEM_LIMIT),
        name="proj",
    )(x, norm_w.reshape(1, D), w_in, gains)


ATT_STEP_ROWS = 4096
ATT_SUB = 2 * RADIUS


def _attn_kernel(q_ref, kp_ref, kc_ref, kn_ref, vp_ref, vc_ref, vn_ref, bias_ref, o_ref, st_ref, kbuf, vbuf):
    nb, ti = q_ref.shape[0], q_ref.shape[1]
    i = pl.program_id(1)
    first = i == 0
    last = i == pl.num_programs(1) - 1
    lane = lax.broadcasted_iota(jnp.int32, (ATT_SUB, LANES), 1)
    low = lane < HEAD_DIM
    ones = jnp.ones((2 * ATT_SUB, LANES), BF16)
    n_sub = ti // ATT_SUB
    for b in range(nb):
        kbuf[b, 0:RADIUS] = kp_ref[b]
        kbuf[b, RADIUS:RADIUS + ti] = kc_ref[b]
        kbuf[b, RADIUS + ti:] = kn_ref[b]
        vbuf[b, 0:RADIUS] = vp_ref[b]
        vbuf[b, RADIUS:RADIUS + ti] = vc_ref[b]
        vbuf[b, RADIUS + ti:] = vn_ref[b]
        for j in range(n_sub):
            r0 = j * ATT_SUB
            variant = 0
            if j == 0:
                variant = jnp.where(first, 1, 0)
            if j == n_sub - 1:
                variant = jnp.where(last, 2, 0)
            st = jnp.zeros((ATT_SUB, LANES), F32)
            for p in range(N_SLOTS // 2):
                c0 = p * LANES
                qp = q_ref[b, r0:r0 + ATT_SUB, c0:c0 + LANES]
                kw = kbuf[b, r0:r0 + 2 * ATT_SUB, c0:c0 + LANES]
                vw = jnp.concatenate([vbuf[b, r0:r0 + 2 * ATT_SUB, c0:c0 + LANES], ones], axis=1)
                zero = jnp.zeros_like(qp)
                qs = jnp.concatenate([jnp.where(low, qp, zero), jnp.where(low, zero, qp)], axis=0)
                s = lax.dot_general(qs, kw, (((1,), (1,)), ((), ())), preferred_element_type=F32)
                s = s + bias_ref[variant, p]
                m = jnp.max(s, axis=-1, keepdims=True)
                e = jnp.exp2(s - m).astype(BF16)
                ol = jnp.dot(e, vw, preferred_element_type=F32)
                o_ref[b, r0:r0 + ATT_SUB, c0:c0 + LANES] = jnp.where(
                    low, ol[:ATT_SUB, :LANES], ol[ATT_SUB:, :LANES]).astype(BF16)
                st = jnp.where(lane == 2 * p, m[:ATT_SUB], st)
                st = jnp.where(lane == 2 * p + 1, m[ATT_SUB:], st)
                st = jnp.where(lane == N_SLOTS + 2 * p, ol[:ATT_SUB, LANES:], st)
                st = jnp.where(lane == N_SLOTS + 2 * p + 1, ol[ATT_SUB:, LANES:], st)
            st_ref[b, r0:r0 + ATT_SUB, :] = st


def _attention(q, k, v, dilation):
    N, L, W = q.shape
    ti = min(L, ATT_STEP_ROWS)
    nb = ATT_STEP_ROWS // ti
    nblk = L // ti
    halo_per_blk = ti // RADIUS
    n_halo = L // RADIUS
    bias = jnp.asarray(_attn_bias(dilation))
    cur = pl.BlockSpec((nb, ti, W), lambda n, i: (n, i, 0))
    prev = pl.BlockSpec((nb, RADIUS, W), lambda n, i: (n, jnp.maximum(i * halo_per_blk - 1, 0), 0))
    nxt = pl.BlockSpec((nb, RADIUS, W), lambda n, i: (n, jnp.minimum((i + 1) * halo_per_blk, n_halo - 1), 0))
    return pl.pallas_call(
        _attn_kernel,
        out_shape=[jax.ShapeDtypeStruct((N, L, W), BF16), jax.ShapeDtypeStruct((N, L, LANES), F32)],
        grid=(N // nb, nblk),
        in_specs=[cur, prev, cur, nxt, prev, cur, nxt,
                  pl.BlockSpec(bias.shape, lambda n, i: (0, 0, 0, 0), pipeline_mode=pl.Buffered(1))],
        out_specs=[pl.BlockSpec((nb, ti, W), lambda n, i: (n, i, 0)),
                   pl.BlockSpec((nb, ti, LANES), lambda n, i: (n, i, 0))],
        scratch_shapes=[pltpu.VMEM((nb, ti + 2 * RADIUS, W), BF16), pltpu.VMEM((nb, ti + 2 * RADIUS, W), BF16)],
        compiler_params=pltpu.CompilerParams(
            dimension_semantics=("parallel", "arbitrary"), vmem_limit_bytes=VMEM_LIMIT),
        name=f"attn_d{dilation}",
    )(q, k, k, k, v, v, v, bias)


FFT_CW = GROUP_W // 2
FFT_UNROLL = NR * NR


def _fft_kernel(u_ref, ga_ref, gb_ref, gc_ref, gj_ref, z_ref, y_ref, y2_ref):
    cw = FFT_CW
    grp = NR * 2 * NJ

    def stage_a(g, carry):
        b, c = g // NR, g % NR
        u = u_ref[0, :, pl.ds(b, 1), pl.ds(c, 1), :, :].reshape(NR, NJ, 2 * cw)
        xg = jnp.concatenate([u[:, :, :cw], u[:, :, cw:]], axis=1).reshape(grp, cw)
        r = jnp.dot(ga_ref[b], xg, preferred_element_type=F32)
        y_ref[pl.ds(b, 1), pl.ds(c, 1)] = r.astype(BF16).reshape(1, 1, NR, 2, NJ, cw)
        return carry

    lax.fori_loop(0, NR * NR, stage_a, 0, unroll=FFT_UNROLL)

    def stage_b(g, carry):
        c, ka = g // NR, g % NR
        xg = y_ref[:, pl.ds(c, 1), pl.ds(ka, 1), :, :, :].reshape(grp, cw)
        r = jnp.dot(gb_ref[c], xg, preferred_element_type=F32)
        y2_ref[:, pl.ds(c, 1), pl.ds(ka, 1), :, :, :] = r.astype(BF16).reshape(NR, 1, 1, 2, NJ, cw)
        return carry

    lax.fori_loop(0, NR * NR, stage_b, 0, unroll=FFT_UNROLL)

    def stage_c(g, carry):
        kb, ka = g // NR, g % NR
        xg = y2_ref[pl.ds(kb, 1), :, pl.ds(ka, 1), :, :, :].reshape(grp, cw)
        r = jnp.dot(gc_ref[ka], xg, preferred_element_type=F32)
        y_ref[pl.ds(kb, 1), :, pl.ds(ka, 1), :, :, :] = r.astype(BF16).reshape(1, NR, 1, 2, NJ, cw)
        return carry

    lax.fori_loop(0, NR * NR, stage_c, 0, unroll=FFT_UNROLL)

    def stage_j(g, carry):
        m, kc = g // NR, g % NR
        xg = y_ref[pl.ds(2 * m, 2), pl.ds(kc, 1)].reshape(2 * grp, cw)
        r = jnp.dot(gj_ref[...], xg, preferred_element_type=F32).astype(BF16)
        z_ref[0, :, pl.ds(kc, 1), pl.ds(m, 1), :, 0:cw] = r[:grp].reshape(NJ, 1, 1, 2 * NR, cw)
        z_ref[0, :, pl.ds(kc, 1), pl.ds(m, 1), :, cw:] = r[grp:].reshape(NJ, 1, 1, 2 * NR, cw)
        return carry

    lax.fori_loop(0, NR * NR // 2, stage_j, 0, unroll=FFT_UNROLL // 2)


def _seq_dft(u):
    B, S, W = u.shape
    mats = [jnp.asarray(g).astype(BF16) for g in _dft_stage_matrices()]
    u6 = u.reshape(B, NR, NR, NR, NJ, W)
    const = lambda shp: pl.BlockSpec(shp, lambda b: (0,) * len(shp), pipeline_mode=pl.Buffered(1))
    z = pl.pallas_call(
        _fft_kernel,
        out_shape=jax.ShapeDtypeStruct((B, NJ, NR, NR // 2, 2 * NR, W), BF16),
        grid=(B,),
        in_specs=[pl.BlockSpec((1, NR, NR, NR, NJ, W), lambda b: (b, 0, 0, 0, 0, 0))] + [const(m.shape) for m in mats],
        out_specs=pl.BlockSpec((1, NJ, NR, NR // 2, 2 * NR, W), lambda b: (b, 0, 0, 0, 0, 0)),
        scratch_shapes=[pltpu.VMEM((NR, NR, NR, 2, NJ, FFT_CW), BF16)] * 2,
        compiler_params=pltpu.CompilerParams(dimension_semantics=("parallel",), vmem_limit_bytes=VMEM_LIMIT),
        name="seq_dft",
    )(u6, *mats)
    return z.reshape(B, S, W)


def _fold_kernel(cd_ref, sd_ref, wf_ref, m_ref, n_ref):
    m_ref[...] = jnp.zeros(m_ref.shape, m_ref.dtype)
    n_ref[...] = jnp.zeros(n_ref.shape, n_ref.dtype)
    for g in range(N_SLOTS):
        h, o = divmod(g, N_SLOTS // 2)
        w = wf_ref[g]
        blk = slice(o * HEAD_DIM, (o + 1) * HEAD_DIM)
        m_ref[h, blk, blk] = (0.5 * jnp.dot(cd_ref[...], w, preferred_element_type=F32,
                                            precision=lax.Precision.HIGHEST)).astype(BF16)
        n_ref[h, blk, blk] = (0.5 * jnp.dot(sd_ref[...], w, preferred_element_type=F32,
                                            precision=lax.Precision.HIGHEST)).astype(BF16)


def _fold_fourier_weights(w_fourier):
    cd, sd = _head_dim_dft()
    shp = jax.ShapeDtypeStruct((2, FFT_CW, FFT_CW), BF16)
    return pl.pallas_call(_fold_kernel, out_shape=[shp, shp], name="fold_fourier")(
        jnp.asarray(cd), jnp.asarray(sd), w_fourier.astype(F32))


FINAL_TM = 1024
PERM_ROWS = 256


def _final_kernel(x_ref, z_ref, zm_ref, zfirst_ref, gf_ref, ga_ref, o0_ref, o1_ref, o2_ref, l0_ref, l1_ref, l2_ref,
                  w1_ref, w2_ref, ex_ref, p1_ref, p2_ref, j1_ref, wo_ref, y_ref, l1_nat, l2_nat):
    tm = FINAL_TM
    n_sub = tm // PERM_ROWS
    n_sub_seq = zfirst_ref.shape[1]
    cw = FFT_CW
    first_row = lax.broadcasted_iota(jnp.int32, (PERM_ROWS, GROUP_W), 0) == 0
    slot_lane = lax.broadcasted_iota(jnp.int32, (tm, LANES), 1) < N_SLOTS

    for l_ref, l_nat, d in ((l1_ref, l1_nat, DILATIONS[1]), (l2_ref, l2_nat, DILATIONS[2])):
        for r in range(d):
            l_nat[pl.ds(r, tm // d, stride=d), :] = l_ref[0, r]

    zms = []
    for s in range(n_sub):
        oth = zm_ref[0, (n_sub - 1 - s) * PERM_ROWS:(n_sub - s) * PERM_ROWS, :]
        g = pl.program_id(1) * n_sub + s
        fix = zfirst_ref[0, pl.ds(lax.rem(n_sub_seq - g, n_sub_seq), 1), 0:1, :].reshape(1, GROUP_W).astype(F32)
        zms.append(jnp.where(first_row, fix, jnp.dot(j1_ref[...], oth, preferred_element_type=F32)))
    own = z_ref[0].astype(F32)
    zm = jnp.concatenate(zms, axis=0)
    zr, zi, mr, mi = own[:, :cw], own[:, cw:], zm[:, :cw], zm[:, cw:]
    halves = (((zr + mr), (zi - mi)), ((zi + mi), (mr - zr)))
    f = jnp.concatenate(
        [jnp.dot(re.astype(BF16), w1_ref[h], preferred_element_type=F32)
         + jnp.dot(im.astype(BF16), w2_ref[h], preferred_element_type=F32)
         for h, (re, im) in enumerate(halves)], axis=-1)
    gf = gf_ref[0].astype(F32)
    yf = (f * (gf * jax.nn.sigmoid(gf))).astype(BF16)

    o_groups = [o0_ref[0].astype(F32)]
    for o_ref, p_ref, d in ((o1_ref, p1_ref, DILATIONS[1]), (o2_ref, p2_ref, DILATIONS[2])):
        n = PERM_ROWS // d
        o_groups.append(jnp.concatenate(
            [jnp.dot(p_ref[...], jnp.concatenate([o_ref[0, r, s * n:(s + 1) * n, :] for r in range(d)], axis=0),
                     preferred_element_type=F32) for s in range(n_sub)], axis=0))
    sts = (l0_ref[0], l1_nat[...], l2_nat[...])
    mx = jnp.maximum(jnp.maximum(sts[0], sts[1]), sts[2])
    es = [jnp.exp2(st - mx) for st in sts]
    den = None
    for e_c, st in zip(es, sts):
        term = e_c * pltpu.roll(st, LANES - N_SLOTS, 1)
        den = term if den is None else den + term
    inv = 1.0 / jnp.where(slot_lane, den, 1.0)
    o = None
    for e_c, o_c in zip(es, o_groups):
        alpha = jnp.dot((e_c * inv).astype(BF16), ex_ref[...], preferred_element_type=F32)
        o = alpha * o_c if o is None else o + alpha * o_c
    ga = ga_ref[0].astype(F32)
    ya = (o * (ga * jax.nn.sigmoid(ga))).astype(BF16)

    mixed = jnp.dot(jnp.concatenate([yf, ya], axis=-1), wo_ref[...], preferred_element_type=F32)
    y_ref[0] = x_ref[0] + mixed


def _final(x, z, gf, ga, o_list, lse_list, w1, w2, w_out):
    B, S, D = x.shape
    tm = FINAL_TM
    nblk = S // tm
    n_sub_seq = S // PERM_ROWS
    d1, d2 = DILATIONS[1], DILATIONS[2]
    z_first = z.reshape(B, n_sub_seq, PERM_ROWS, GROUP_W)
    nat = lambda w: pl.BlockSpec((1, tm, w), lambda b, i: (b, i, 0))
    perm = lambda d, w: pl.BlockSpec((1, d, tm // d, w), lambda b, i: (b, 0, i, 0))
    const = lambda shp: pl.BlockSpec(shp, lambda b, i: (0,) * len(shp), pipeline_mode=pl.Buffered(1))
    p1 = jnp.asarray(_unpermute_matrix(PERM_ROWS, d1)).astype(BF16)
    p2 = jnp.asarray(_unpermute_matrix(PERM_ROWS, d2)).astype(BF16)
    j1 = jnp.asarray(_mirror_matrix(PERM_ROWS)).astype(BF16)
    ex = jnp.asarray(_slot_expand_matrix()).astype(BF16)
    return pl.pallas_call(
        _final_kernel,
        out_shape=jax.ShapeDtypeStruct((B, S, D), F32),
        grid=(B, nblk),
        in_specs=[nat(D), nat(GROUP_W), pl.BlockSpec((1, tm, GROUP_W), lambda b, i: (b, nblk - 1 - i, 0)),
                  pl.BlockSpec((1, n_sub_seq, 16, GROUP_W), lambda b, i: (b, 0, 0, 0)),
                  nat(GROUP_W), nat(GROUP_W),
                  nat(GROUP_W), perm(d1, GROUP_W), perm(d2, GROUP_W), nat(LANES), perm(d1, LANES), perm(d2, LANES),
                  const((2, FFT_CW, FFT_CW)), const((2, FFT_CW, FFT_CW)), const((LANES, GROUP_W)),
                  const((PERM_ROWS, PERM_ROWS)), const((PERM_ROWS, PERM_ROWS)), const((PERM_ROWS, PERM_ROWS)),
                  const((D, D))],
        out_specs=nat(D),
        scratch_shapes=[pltpu.VMEM((tm, LANES), F32), pltpu.VMEM((tm, LANES), F32)],
        compiler_params=pltpu.CompilerParams(
            dimension_semantics=("parallel", "arbitrary"), vmem_limit_bytes=VMEM_LIMIT),
        name="final_mix",
    )(x, z, z, z_first, gf, ga, *o_list, *lse_list, w1, w2, ex, p1, p2, j1, w_out)


def kernel(x, norm_w, w_in, q_norm_w, k_norm_w, w_fourier, w_out):
    B, S, D = x.shape
    n_cfg = len(DILATIONS)
    xf = x.astype(F32)
    gains = jnp.concatenate([q_norm_w.astype(F32).reshape(n_cfg, GROUP_W) * (HEAD_DIM ** -0.5 * LOG2E),
                             k_norm_w.astype(F32).reshape(n_cfg, GROUP_W)], axis=0)
    proj = _project(xf, norm_w.astype(F32), w_in.astype(BF16), gains)
    u_f, g_f, g_a = proj[:3]

    outs, lses = [], []
    for c, d in enumerate(DILATIONS):
        q, k, v = (t.reshape(B * d, S // d, GROUP_W) for t in proj[3 + 3 * c:6 + 3 * c])
        o, lse = _attention(q, k, v, d)
        if d == 1:
            outs.append(o)
            lses.append(lse)
        else:
            outs.append(o.reshape(B, d, S // d, GROUP_W))
            lses.append(lse.reshape(B, d, S // d, LANES))

    z = _seq_dft(u_f)
    w1, w2 = _fold_fourier_weights(w_fourier)
    y = _final(xf, z, g_f, g_a, outs, lses, w1, w2, w_out.astype(BF16))
    return y.astype(x.dtype)
```
